```python
import math
import jax
import jax.numpy as jnp
from jax import lax
import numpy as np

D_MODEL = 1024
BATCH = 8
SEQ = 4096
DEPTH = 2

N_META = 16
CHUNK = 128
PAD = CHUNK - N_META
HEAD_DIM = 64
SB_HEADS = 4
SB_W = SB_HEADS * HEAD_DIM
SSD_HEADS = 8
SSD_W = SSD_HEADS * HEAD_DIM
SSD_GROUPS = 2
SSD_STATE = 128
SSD_CONV = 4
SSD_CONV_DIM = SSD_W + 2 * SSD_GROUPS * SSD_STATE
HG_HEADS = 4
HG_DK = 64
HG_DV = 64
HG_W = HG_HEADS * HG_DV
D_MIX = SB_W + SSD_W + HG_W
D_FF = 4 * D_MODEL
EPS = 1e-6
TINY = 1e-30
IN_SIZES = (SB_W, SB_W, SB_W,
            SSD_W, SSD_W, SSD_GROUPS * SSD_STATE, SSD_GROUPS * SSD_STATE, SSD_HEADS,
            HG_HEADS * HG_DK, HG_HEADS * HG_DK, HG_W, HG_W)
D_IN = 3 * SB_W + 2 * SSD_W + 2 * SSD_GROUPS * SSD_STATE + SSD_HEADS + 2 * HG_HEADS * HG_DK + 2 * HG_W

kernel_name = 'hymba_sb_ssd_hgrn2_block'


def rmsnorm(x, w):
    xf = x.astype(jnp.float32)
    y = xf * lax.rsqrt(jnp.mean(xf * xf, axis=-1, keepdims=True) + EPS)
    return (y * w.astype(jnp.float32)).astype(x.dtype)


def causal_depthwise_conv(u, w, b):
    y = lax.conv_general_dilated(
        u, w[:, None, :].astype(u.dtype), window_strides=(1,), padding=[(w.shape[0] - 1, 0)],
        dimension_numbers=('NWC', 'WIO', 'NWC'), feature_group_count=u.shape[-1])
    return y + b.astype(u.dtype)


def to_chunks(t):
    b, l = t.shape[:2]
    return jnp.moveaxis(t.reshape((b, l // CHUNK, CHUNK) + t.shape[2:]), 1, 0)


def from_chunks(t):
    n, b = t.shape[:2]
    return jnp.moveaxis(t, 0, 1).reshape((b, n * CHUNK) + t.shape[3:])


def masked_decay(seg, mask):
    return jnp.where(mask, jnp.exp(jnp.where(mask, seg, 0.0)), 0.0)


def stick_breaking_attention(q, k, v, valid):
    L, dh = q.shape[2], q.shape[3]
    scale = dh ** -0.5
    pos = jnp.arange(L)
    outs = []
    for blk in range(L // CHUNK):
        start, end = blk * CHUNK, (blk + 1) * CHUNK
        z = jnp.einsum('bhqd,bhkd->bhqk', q[:, :, start:end], k[:, :, :end]).astype(jnp.float32) * scale
        mask = (pos[None, :end] < pos[start:end, None]) & valid[None, :end]
        log_keep = jnp.where(mask, jax.nn.log_sigmoid(-z), 0.0)
        csum = jnp.cumsum(log_keep, axis=-1)
        log_w = jax.nn.log_sigmoid(z) + (csum[..., -1:] - csum)
        w = jnp.where(mask, jnp.exp(jnp.where(mask, log_w, 0.0)), 0.0)
        outs.append(jnp.einsum('bhqk,bhkd->bhqd', w.astype(v.dtype), v[:, :, :end]))
    return jnp.concatenate(outs, axis=2)


def stick_breaking_group(q, k, v, q_norm, k_norm, out_norm, valid):
    bsz, L, _ = q.shape
    shp = (bsz, L, SB_HEADS, HEAD_DIM)
    qh = jnp.transpose(rmsnorm(q.reshape(shp), q_norm), (0, 2, 1, 3))
    kh = jnp.transpose(rmsnorm(k.reshape(shp), k_norm), (0, 2, 1, 3))
    vh = jnp.transpose(v.reshape(shp), (0, 2, 1, 3))
    o = jnp.transpose(stick_breaking_attention(qh, kh, vh, valid), (0, 2, 1, 3))
    return rmsnorm(o, out_norm).reshape(bsz, L, SB_W)


def ssd_chunked(xdt, a, bm, cm):
    bsz = xdt.shape[0]
    rep = SSD_HEADS // SSD_GROUPS
    causal = jnp.tril(jnp.ones((CHUNK, CHUNK), dtype=bool))

    def step(state, inp):
        a_c, x_c, b_c, c_c = inp
        acum = jnp.cumsum(a_c, axis=1)
        seg = acum[:, :, None, :] - acum[:, None, :, :]
        decay = masked_decay(seg, causal[None, :, :, None])
        cb = jnp.repeat(jnp.einsum('btgn,bsgn->btsg', c_c, b_c), rep, axis=-1)
        y_diag = jnp.einsum('btsh,bshp->bthp', cb * decay, x_c)
        c_h = jnp.repeat(c_c, rep, axis=2)
        b_h = jnp.repeat(b_c, rep, axis=2)
        y_off = jnp.einsum('bthn,bhpn->bthp', c_h, state) * jnp.exp(acum)[..., None]
        w_end = jnp.exp(acum[:, -1:, :] - acum)
        state = state * jnp.exp(acum[:, -1, :])[:, :, None, None] + jnp.einsum('bshn,bsh,bshp->bhpn', b_h, w_end, x_c)
        return state, y_diag + y_off

    state0 = jnp.zeros((bsz, SSD_HEADS, HEAD_DIM, SSD_STATE), jnp.float32)
    xs = tuple(to_chunks(t.astype(jnp.float32)) for t in (a, xdt, bm, cm))
    _, y = lax.scan(step, state0, xs)
    return from_chunks(y)


def ssd_group(z, xs, bm, cm, dt_raw, conv_w, conv_b, dt_bias, a_log, d_skip, norm_w, vmask):
    bsz, L, _ = xs.shape
    f32 = jnp.float32
    xbc = jax.nn.silu(causal_depthwise_conv(jnp.concatenate([xs, bm, cm], axis=-1) * vmask, conv_w, conv_b))
    xs, bm, cm = jnp.split(xbc, [SSD_W, SSD_W + SSD_GROUPS * SSD_STATE], axis=-1)
    dt = jax.nn.softplus(dt_raw.astype(f32) + dt_bias.astype(f32)) * vmask.astype(f32)
    a_neg = -jnp.exp(a_log.astype(f32))
    xh = xs.reshape(bsz, L, SSD_HEADS, HEAD_DIM).astype(f32)
    y = ssd_chunked(xh * dt[..., None], dt * a_neg,
                    bm.reshape(bsz, L, SSD_GROUPS, SSD_STATE), cm.reshape(bsz, L, SSD_GROUPS, SSD_STATE))
    y = y + xh * d_skip.astype(f32)[:, None]
    y = y.reshape(bsz, L, SSD_W) * jax.nn.silu(z.astype(f32))
    y = rmsnorm(y.reshape(bsz, L, SSD_GROUPS, SSD_W // SSD_GROUPS), norm_w)
    return y.reshape(bsz, L, SSD_W).astype(z.dtype)


def hgrn2_chunked(q, k, v, log_f):
    bsz = q.shape[0]
    causal = jnp.tril(jnp.ones((CHUNK, CHUNK), dtype=bool))

    def step(S, inp):
        q_c, k_c, v_c, g_c = inp
        gcum = jnp.cumsum(g_c, axis=1)
        seg = gcum[:, :, None] - gcum[:, None, :]
        decay = masked_decay(seg, causal[None, :, :, None, None])
        scores = jnp.einsum('bthk,btshk,bshk->btsh', q_c, decay, k_c)
        o_intra = jnp.einsum('btsh,bshv->bthv', scores, v_c)
        o_inter = jnp.einsum('bthk,bhkv->bthv', q_c * jnp.exp(gcum), S)
        k_end = k_c * jnp.exp(gcum[:, -1:] - gcum)
        S = S * jnp.exp(gcum[:, -1])[..., None] + jnp.einsum('bshk,bshv->bhkv', k_end, v_c)
        return S, o_intra + o_inter

    S0 = jnp.zeros((bsz, HG_HEADS, HG_DK, HG_DV), jnp.float32)
    _, o = lax.scan(step, S0, tuple(to_chunks(t) for t in (q, k, v, log_f)))
    return from_chunks(o)


def hgrn2_group(q, f_logit, i_in, g, lb, out_norm, valid):
    bsz, L, _ = q.shape
    f32 = jnp.float32
    fl = f_logit.astype(f32)
    keep = valid[None, :, None]
    f = lb + (1.0 - lb) * jax.nn.sigmoid(fl)
    log_f = jnp.where(keep, jnp.log(jnp.maximum(f, TINY)), 0.0)
    k = jnp.where(keep, (1.0 - lb) * jax.nn.sigmoid(-fl), 0.0)
    v = jnp.where(keep, i_in.astype(f32), 0.0)
    qf = jax.nn.silu(q.astype(f32))
    shp = (bsz, L, HG_HEADS, HG_DK)
    o = hgrn2_chunked(qf.reshape(shp), k.reshape(shp), v.reshape(bsz, L, HG_HEADS, HG_DV), log_f.reshape(shp))
    o = rmsnorm(o, out_norm) * jax.nn.silu(g.astype(f32)).reshape(bsz, L, HG_HEADS, HG_DV)
    return o.reshape(bsz, L, HG_W).astype(q.dtype)


def setup_inputs(seed: int = 0) -> dict:
    key = jax.random.key(seed)
    ks = jax.random.split(key, 19)
    f32 = jnp.float32

    def normal(k, shape, scale):
        return scale * jax.random.normal(k, shape, f32)

    def gain(k, shape):
        return 1.0 + 0.01 * jax.random.normal(k, shape, f32)

    dt_init = jnp.exp(jax.random.uniform(ks[10], (DEPTH, SSD_HEADS), f32,
                                         minval=math.log(1e-3), maxval=math.log(1e-1)))
    return {
        'x': normal(ks[0], (BATCH, SEQ, D_MODEL), 1.0),
        'meta_tokens': normal(ks[1], (N_META, D_MODEL), 1.0),
        'hg_lb_logits': normal(ks[2], (DEPTH, HG_HEADS * HG_DK), 0.5),
        'norm_mix_w': gain(ks[3], (DEPTH, D_MODEL)),
        'w_in': normal(ks[4], (DEPTH, D_MODEL, D_IN), D_MODEL ** -0.5),
        'sb_q_norm': gain(ks[5], (DEPTH, HEAD_DIM)),
        'sb_k_norm': gain(ks[6], (DEPTH, HEAD_DIM)),
        'sb_out_norm': gain(ks[7], (DEPTH, SB_HEADS, HEAD_DIM)),
        'ssd_conv_w': normal(ks[8], (DEPTH, SSD_CONV, SSD_CONV_DIM), SSD_CONV ** -0.5),
        'ssd_conv_b': normal(ks[9], (DEPTH, SSD_CONV_DIM), 0.01),
        'ssd_dt_bias': dt_init + jnp.log(-jnp.expm1(-dt_init)),
        'ssd_A_log': jnp.log(jax.random.uniform(ks[11], (DEPTH, SSD_HEADS), f32, minval=1.0, maxval=16.0)),
        'ssd_D': gain(ks[12], (DEPTH, SSD_HEADS)),
        'ssd_norm_w': gain(ks[13], (DEPTH, SSD_GROUPS, SSD_W // SSD_GROUPS)),
        'hg_out_norm': gain(ks[14], (DEPTH, HG_HEADS, HG_DV)),
        'w_out': normal(ks[15], (DEPTH, D_MIX, D_MODEL), D_MIX ** -0.5),
        'norm_mlp_w': gain(ks[16], (DEPTH, D_MODEL)),
        'w_up': normal(ks[17], (DEPTH, D_MODEL, D_FF), D_MODEL ** -0.5),
        'w_down': normal(ks[18], (DEPTH, D_FF, D_MODEL), D_FF ** -0.5),
    }


def reference(x, meta_tokens, hg_lb_logits, norm_mix_w, w_in, sb_q_norm, sb_k_norm, sb_out_norm,
              ssd_conv_w, ssd_conv_b, ssd_dt_bias, ssd_A_log, ssd_D, ssd_norm_w, hg_out_norm,
              w_out, norm_mlp_w, w_up, w_down):
    bsz = x.shape[0]
    dtype = x.dtype
    lead = jnp.concatenate([jnp.zeros((PAD, D_MODEL), dtype), meta_tokens.astype(dtype)], axis=0)
    h = jnp.concatenate([jnp.broadcast_to(lead[None], (bsz, CHUNK, D_MODEL)), x], axis=1)
    L = h.shape[1]
    valid = jnp.arange(L) >= PAD
    vmask = valid[None, :, None].astype(dtype)
    probs = jax.nn.softmax(hg_lb_logits.astype(jnp.float32), axis=0)
    lbs = jnp.concatenate([jnp.zeros_like(probs[0:1]), jnp.cumsum(probs, axis=0)[:-1]], axis=0)
    split_at = np.cumsum(IN_SIZES)[:-1].tolist()
    for l in range(DEPTH):
        hn = rmsnorm(h, norm_mix_w[l])
        proj = hn @ w_in[l]
        (q_sb, k_sb, v_sb, z_ssd, x_ssd, b_ssd, c_ssd, dt_ssd,
         q_hg, f_hg, i_hg, g_hg) = jnp.split(proj, split_at, axis=-1)
        o_sb = stick_breaking_group(q_sb, k_sb, v_sb, sb_q_norm[l], sb_k_norm[l], sb_out_norm[l], valid)
        o_ssd = ssd_group(z_ssd, x_ssd, b_ssd, c_ssd, dt_ssd, ssd_conv_w[l], ssd_conv_b[l], ssd_dt_bias[l],
                          ssd_A_log[l], ssd_D[l], ssd_norm_w[l], vmask)
        o_hg = hgrn2_group(q_hg, f_hg, i_hg, g_hg, lbs[l], hg_out_norm[l], valid)
        h = h + jnp.concatenate([o_sb, o_ssd, o_hg], axis=-1) @ w_out[l]
        hn = rmsnorm(h, norm_mlp_w[l])
        h = h + jnp.square(jax.nn.relu(hn @ w_up[l])) @ w_down[l]
    return h[:, CHUNK:]
```

```python
import functools

import numpy as np
import jax
import jax.numpy as jnp
from jax import lax
from jax.experimental import pallas as pl
from jax.experimental.pallas import tpu as pltpu

F32 = jnp.float32
BF16 = jnp.bfloat16

N_META = 16
CHUNK = 128
PAD = CHUNK - N_META
HEAD_DIM = 64
SB_HEADS = 4
SB_W = SB_HEADS * HEAD_DIM
SSD_HEADS = 8
SSD_W = SSD_HEADS * HEAD_DIM
SSD_GROUPS = 2
SSD_STATE = 128
SSD_CONV = 4
SSD_BC_W = SSD_GROUPS * SSD_STATE
SSD_CONV_DIM = SSD_W + 2 * SSD_BC_W
HG_HEADS = 4
HG_DK = 64
HG_W = HG_HEADS * HG_DK
EPS = 1e-6
TINY = 1e-30
SUB = 16
LANES = 128
DT_W = LANES
FF_BLOCK = 1024
VMEM_CAP_V7X = 64 * 1024 * 1024

_C_Q, _C_K, _C_V = 0, SB_W, 2 * SB_W
_C_Z = 3 * SB_W
_C_XBC = _C_Z + SSD_W
_C_DT = _C_XBC + SSD_CONV_DIM
_C_HQ = _C_DT + DT_W
_C_HF = _C_HQ + HG_W
_C_HI = _C_HF + HG_W
_C_HG = _C_HI + HG_W
D_IN_PACKED = _C_HG + HG_W


def _vmem_limit(need_bytes):
    return int(min(max(need_bytes, 32 * 1024 * 1024), VMEM_CAP_V7X - 6 * 1024 * 1024))


def _row_tile(rows, target):
    t = min(target, rows)
    while rows % t:
        t -= CHUNK
    return t


def _dot(a, b):
    return jnp.dot(a, b, preferred_element_type=F32)


def _dot_nt(a, b):
    return lax.dot_general(a, b, (((1,), (1,)), ((), ())), preferred_element_type=F32)


def _dot_tn(a, b):
    return lax.dot_general(a, b, (((0,), (0,)), ((), ())), preferred_element_type=F32)


def _split2(x):
    hi = x.astype(BF16)
    lo = (x - hi.astype(F32)).astype(BF16)
    return hi, lo


def _split3(x):
    hi = x.astype(BF16)
    r = x - hi.astype(F32)
    mid = r.astype(BF16)
    lo = (r - mid.astype(F32)).astype(BF16)
    return hi, mid, lo


def _dot_f32_right(x, m):
    hi, lo = _split2(x)
    return _dot(hi, m) + _dot(lo, m)


def _dot_f32_left3(m, x):
    hi, mid, lo = _split3(x)
    return _dot(m, hi) + _dot(m, mid) + _dot(m, lo)


def _softplus(x):
    return jnp.maximum(x, 0.0) + jnp.log1p(jnp.exp(-jnp.abs(x)))


def _sigmoid(x):
    return 1.0 / (1.0 + jnp.exp(-x))


def _silu(x):
    return x * _sigmoid(x)


def _in_proj_body(h_ref, nw_ref, w_ref, qn_ref, kn_ref, bd_ref,
                  q_ref, k_ref, v_ref, z_ref, xbc_ref, dt_ref, hq_ref, hf_ref, hi_ref, hg_ref):
    x = h_ref[...]
    ms = jnp.mean(x * x, axis=-1, keepdims=True)
    hn = (x * lax.rsqrt(ms + EPS) * nw_ref[...]).astype(BF16)

    def seg(lo, width):
        return _dot(hn, w_ref[:, lo:lo + width])

    def head_norm(t, w):
        hms = _dot_f32_right(t * t, bd_ref[...]) * (1.0 / HEAD_DIM)
        return t * lax.rsqrt(hms + EPS) * w

    q = head_norm(seg(_C_Q, SB_W), qn_ref[...])
    q_ref[...] = (q * (HEAD_DIM ** -0.5)).astype(BF16)
    k_ref[...] = head_norm(seg(_C_K, SB_W), kn_ref[...]).astype(BF16)
    v_ref[...] = seg(_C_V, SB_W).astype(BF16)
    z_ref[...] = seg(_C_Z, SSD_W)
    xbc_ref[...] = seg(_C_XBC, SSD_CONV_DIM)
    dt_ref[...] = seg(_C_DT, DT_W)
    hq_ref[...] = seg(_C_HQ, HG_W)
    hf_ref[...] = seg(_C_HF, HG_W)
    hi_ref[...] = seg(_C_HI, HG_W)
    hg_ref[...] = seg(_C_HG, HG_W)


def _in_proj(h, norm_w, w_packed, qn, kn, bd256):
    rows, d = h.shape
    tm = _row_tile(rows, 512)
    widths = (SB_W, SB_W, SB_W, SSD_W, SSD_CONV_DIM, DT_W, HG_W, HG_W, HG_W, HG_W)
    dtypes = (BF16, BF16, BF16, F32, F32, F32, F32, F32, F32, F32)
    const = lambda i: (0, 0)
    out_bytes = sum(w * jnp.dtype(t).itemsize for w, t in zip(widths, dtypes)) * tm
    need = 2 * (tm * d * 4 + d * D_IN_PACKED * 2 + out_bytes) + tm * SSD_CONV_DIM * 4 * 4
    return pl.pallas_call(
        _in_proj_body,
        grid=(rows // tm,),
        in_specs=[
            pl.BlockSpec((tm, d), lambda i: (i, 0)),
            pl.BlockSpec((1, d), const),
            pl.BlockSpec((d, D_IN_PACKED), const),
            pl.BlockSpec((1, SB_W), const),
            pl.BlockSpec((1, SB_W), const),
            pl.BlockSpec((SB_W, SB_W), const),
        ],
        out_specs=[pl.BlockSpec((tm, w), lambda i: (i, 0)) for w in widths],
        out_shape=[jax.ShapeDtypeStruct((rows, w), t) for w, t in zip(widths, dtypes)],
        compiler_params=pltpu.CompilerParams(
            dimension_semantics=("parallel",), vmem_limit_bytes=_vmem_limit(need)),
        name="in_proj",
    )(h, norm_w, w_packed, qn, kn, bd256)


def _sb_body(q_ref, k_ref, v_ref, onw_ref, usum_ref, bd_ref, o_ref, acc_ref, rb_ref):
    i = pl.program_id(2)
    lane = lax.broadcasted_iota(jnp.int32, (CHUNK, LANES), 1)
    row = lax.broadcasted_iota(jnp.int32, (CHUNK, CHUNK), 0)
    col = lax.broadcasted_iota(jnp.int32, (CHUNK, CHUNK), 1)
    q = q_ref[...]
    zero = jnp.zeros_like(q)
    q_heads = (jnp.where(lane < HEAD_DIM, q, zero), jnp.where(lane >= HEAD_DIM, q, zero))
    acc_ref[...] = jnp.zeros_like(acc_ref)
    rb_ref[...] = jnp.zeros_like(rb_ref)

    def key_mask(j):
        key_pos = j * CHUNK + col
        return (key_pos < i * CHUNK + row) & (key_pos >= PAD)

    def tile(j, mask):
        off = j * CHUNK if isinstance(j, int) else pl.multiple_of(j * CHUNK, CHUNK)
        kb = k_ref[pl.ds(off, CHUNK), :]
        vb = v_ref[pl.ds(off, CHUNK), :]
        for h in range(2):
            z = _dot_nt(q_heads[h], kb)
            sp = _softplus(z)
            log_keep = -sp
            if mask is not None:
                log_keep = jnp.where(mask, log_keep, 0.0)
            sums = _dot_f32_right(log_keep, usum_ref[...])
            rb = rb_ref[h]
            log_w = (z - sp) + sums[:, :CHUNK] + rb
            if mask is None:
                w = jnp.exp(log_w)
            else:
                w = jnp.where(mask, jnp.exp(jnp.where(mask, log_w, 0.0)), 0.0)
            rb_ref[h] = rb + sums[:, CHUNK:]
            acc_ref[h] += _dot(w.astype(BF16), vb)

    tile(i, key_mask(i))

    def body(jj, carry):
        tile(i - 1 - jj, None)
        return carry

    lax.fori_loop(0, jnp.maximum(i - 1, 0), body, 0)

    @pl.when(i > 0)
    def _():
        tile(0, key_mask(0))

    o = jnp.where(lane < HEAD_DIM, acc_ref[0], acc_ref[1])
    hms = _dot_f32_right(o * o, bd_ref[...]) * (1.0 / HEAD_DIM)
    o_ref[...] = (o * lax.rsqrt(hms + EPS) * onw_ref[...]).astype(BF16)


def _sb_attn(q, k, v, out_norm, usum, bd128, bsz, length):
    nc = length // CHUNK
    const = lambda b, p, i: (0, 0)
    need = 2 * (2 * length * LANES * 2) + 8 * CHUNK * CHUNK * 4 * 8
    return pl.pallas_call(
        _sb_body,
        grid=(bsz, SB_W // LANES, nc),
        in_specs=[
            pl.BlockSpec((CHUNK, LANES), lambda b, p, i: (b * nc + i, p)),
            pl.BlockSpec((length, LANES), lambda b, p, i: (b, p)),
            pl.BlockSpec((length, LANES), lambda b, p, i: (b, p)),
            pl.BlockSpec((1, LANES), lambda b, p, i: (0, p)),
            pl.BlockSpec((CHUNK, 2 * CHUNK), const),
            pl.BlockSpec((LANES, LANES), const),
        ],
        out_specs=pl.BlockSpec((CHUNK, LANES), lambda b, p, i: (b * nc + i, p)),
        out_shape=jax.ShapeDtypeStruct((bsz * length, SB_W), BF16),
        scratch_shapes=[pltpu.VMEM((2, CHUNK, LANES), F32), pltpu.VMEM((2, CHUNK, CHUNK), F32)],
        compiler_params=pltpu.CompilerParams(
            dimension_semantics=("parallel", "parallel", "arbitrary"), vmem_limit_bytes=_vmem_limit(need)),
        name="sb_attn",
    )(q, k, v, out_norm, usum, bd128)


def _ssd_body(z_ref, xbc_ref, dt_ref, cw_ref, cb_ref, dtb_ref, alog_ref, dexp_ref, nw_ref, ltri_ref, eexp_ref,
              o_ref, buf_ref, st_ref):
    c = pl.program_id(1)
    heads_per_group = SSD_HEADS // SSD_GROUPS
    group_w = SSD_W // SSD_GROUPS
    tail = SSD_CONV - 1

    @pl.when(c == 0)
    def _():
        st_ref[...] = jnp.zeros_like(st_ref)
        buf_ref[0:8, :] = jnp.zeros((8, SSD_CONV_DIM), F32)

    rowi = lax.broadcasted_iota(jnp.int32, (CHUNK, 1), 0)
    valid = (c * CHUNK + rowi) >= PAD
    buf_ref[8:8 + CHUNK, :] = jnp.where(valid, xbc_ref[...], 0.0)
    conv = cb_ref[...]
    for i in range(SSD_CONV):
        conv = conv + cw_ref[i:i + 1, :] * buf_ref[8 - tail + i:8 - tail + i + CHUNK, :]
    buf_ref[0:8, :] = buf_ref[CHUNK:CHUNK + 8, :]
    act = _silu(conv)
    xs = act[:, :SSD_W]
    bm = act[:, SSD_W:SSD_W + SSD_BC_W].astype(BF16)
    cm = act[:, SSD_W + SSD_BC_W:].astype(BF16)

    dt = jnp.where(valid, _softplus(dt_ref[...] + dtb_ref[...]), 0.0)
    a = dt * (-jnp.exp(alog_ref[...]))
    acum = _dot_f32_left3(ltri_ref[...], a)
    acum_t = acum.T
    a_last = acum[CHUNK - 1:CHUNK, :]
    per_head = jnp.concatenate([dt, jnp.exp(acum), jnp.exp(a_last - acum)], axis=0)
    expanded = _dot_f32_right(per_head, eexp_ref[...])
    dt_e = expanded[:CHUNK]
    decay_in_e = expanded[CHUNK:2 * CHUNK]
    decay_out_e = expanded[2 * CHUNK:]
    xdt = xs * dt_e
    xdt_b = xdt.astype(BF16)
    xw_b = (xdt * decay_out_e).astype(BF16)

    row = lax.broadcasted_iota(jnp.int32, (CHUNK, CHUNK), 0)
    col = lax.broadcasted_iota(jnp.int32, (CHUNK, CHUNK), 1)
    causal = row >= col
    lane = lax.broadcasted_iota(jnp.int32, (CHUNK, LANES), 1)
    upper_half = lane >= HEAD_DIM

    ys = []
    for g in range(SSD_GROUPS):
        cg = cm[:, g * SSD_STATE:(g + 1) * SSD_STATE]
        bg = bm[:, g * SSD_STATE:(g + 1) * SSD_STATE]
        gcols = slice(g * group_w, (g + 1) * group_w)
        cb = _dot_nt(cg, bg)
        st = st_ref[g]
        y_off = _dot(cg, st.astype(BF16)) * decay_in_e[:, gcols]
        pairs = []
        for pr in range(heads_per_group // 2):
            xp = xdt_b[:, g * group_w + pr * LANES:g * group_w + (pr + 1) * LANES]
            acc = None
            for hh in range(2):
                h = g * heads_per_group + pr * 2 + hh
                seg = acum[:, h:h + 1] - acum_t[h:h + 1, :]
                decay = jnp.where(causal, jnp.exp(jnp.where(causal, seg, 0.0)), 0.0)
                m = (cb * decay).astype(BF16)
                keep = upper_half if hh else jnp.logical_not(upper_half)
                t = _dot(m, jnp.where(keep, xp, jnp.zeros_like(xp)))
                acc = t if acc is None else acc + t
            pairs.append(acc)
        ys.append(jnp.concatenate(pairs, axis=1) + y_off)
        st_ref[g] = st * decay_in_e[CHUNK - 1:CHUNK, gcols] + _dot_tn(bg, xw_b[:, gcols])

    y = jnp.concatenate(ys, axis=1) + xs * dexp_ref[...]
    y = y * _silu(z_ref[...])
    outs = []
    for g in range(SSD_GROUPS):
        yg = y[:, g * group_w:(g + 1) * group_w]
        gms = jnp.mean(yg * yg, axis=-1, keepdims=True)
        outs.append(yg * lax.rsqrt(gms + EPS) * nw_ref[:, g * group_w:(g + 1) * group_w])
    o_ref[...] = jnp.concatenate(outs, axis=1).astype(BF16)


def _ssd(z, xbc, dt, conv_w, conv_b, dt_bias, a_log, d_exp, norm_w, ltri, eexp, bsz, length):
    nc = length // CHUNK
    const = lambda b, c: (0, 0)
    rows = lambda b, c: (b * nc + c, 0)
    group_w = SSD_W // SSD_GROUPS
    need = 2 * CHUNK * (SSD_W + SSD_CONV_DIM + DT_W) * 4 * 2 + 64 * CHUNK * SSD_CONV_DIM * 4
    return pl.pallas_call(
        _ssd_body,
        grid=(bsz, nc),
        in_specs=[
            pl.BlockSpec((CHUNK, SSD_W), rows),
            pl.BlockSpec((CHUNK, SSD_CONV_DIM), rows),
            pl.BlockSpec((CHUNK, DT_W), rows),
            pl.BlockSpec((SSD_CONV, SSD_CONV_DIM), const),
            pl.BlockSpec((1, SSD_CONV_DIM), const),
            pl.BlockSpec((1, DT_W), const),
            pl.BlockSpec((1, DT_W), const),
            pl.BlockSpec((1, SSD_W), const),
            pl.BlockSpec((1, SSD_W), const),
            pl.BlockSpec((CHUNK, CHUNK), const),
            pl.BlockSpec((DT_W, SSD_W), const),
        ],
        out_specs=pl.BlockSpec((CHUNK, SSD_W), rows),
        out_shape=jax.ShapeDtypeStruct((bsz * length, SSD_W), BF16),
        scratch_shapes=[pltpu.VMEM((CHUNK + 8, SSD_CONV_DIM), F32),
                        pltpu.VMEM((SSD_GROUPS, SSD_STATE, group_w), F32)],
        compiler_params=pltpu.CompilerParams(
            dimension_semantics=("parallel", "arbitrary"), vmem_limit_bytes=_vmem_limit(need)),
        name="ssd",
    )(z, xbc, dt, conv_w, conv_b, dt_bias, a_log, d_exp, norm_w, ltri, eexp)


def _hg_body(q_ref, f_ref, i_ref, g_ref, lb_ref, onw_ref, ltri_ref, bd_ref,
             o_ref, st_ref, qs_ref, ks_ref, vs_ref, gs_ref, oacc_ref):
    c = pl.program_id(1)
    nsub = CHUNK // SUB

    @pl.when(c == 0)
    def _():
        st_ref[...] = jnp.zeros_like(st_ref)

    rowi = lax.broadcasted_iota(jnp.int32, (CHUNK, 1), 0)
    valid = (c * CHUNK + rowi) >= PAD
    lb = lb_ref[...]
    fl = f_ref[...]
    f = lb + (1.0 - lb) * _sigmoid(fl)
    log_f = jnp.where(valid, jnp.log(jnp.maximum(f, TINY)), 0.0)
    k = jnp.where(valid, (1.0 - lb) * _sigmoid(-fl), 0.0)
    v = jnp.where(valid, i_ref[...], 0.0)
    q = _silu(q_ref[...])
    gc = _dot_f32_left3(ltri_ref[...], log_f)
    qs_ref[...] = q
    ks_ref[...] = k
    vs_ref[...] = v
    gs_ref[...] = gc
    g_last = gc[CHUNK - 1:CHUNK, :]

    st = st_ref[...]
    oacc_ref[...] = _dot_nt((q * jnp.exp(gc)).astype(BF16), st.astype(BF16))
    k_end = (k * jnp.exp(g_last - gc)).astype(BF16)
    srow = lax.broadcasted_iota(jnp.int32, (HG_W, HG_W), 0) // HG_DK
    scol = lax.broadcasted_iota(jnp.int32, (HG_W, HG_W), 1) // HG_DK
    st_ref[...] = st * jnp.exp(g_last) + jnp.where(srow == scol, _dot_tn(v.astype(BF16), k_end), 0.0)

    erow = lax.broadcasted_iota(jnp.int32, (HG_HEADS * SUB, HG_W), 0) // SUB
    ecol = lax.broadcasted_iota(jnp.int32, (HG_HEADS * SUB, HG_W), 1) // HG_DK
    same_head = erow == ecol
    for jb in range(nsub - 1):
        r0, r1 = jb * SUB, (jb + 1) * SUB
        g_end = gc[r1 - 1:r1, :]
        ke = k[r0:r1] * jnp.exp(g_end - gc[r0:r1])
        ke4 = jnp.where(same_head, jnp.concatenate([ke] * HG_HEADS, axis=0), 0.0).astype(BF16)
        v4 = jnp.where(same_head, jnp.concatenate([v[r0:r1]] * HG_HEADS, axis=0), 0.0).astype(BF16)
        qp = (q[r1:] * jnp.exp(gc[r1:] - g_end)).astype(BF16)
        scores = _dot_nt(qp, ke4)
        oacc_ref[r1:, :] += _dot(scores.astype(BF16), v4)

    rin = lax.broadcasted_iota(jnp.int32, (SUB, 1), 0)

    def diag_block(ib, carry):
        r0 = pl.multiple_of(ib * SUB, SUB)
        qi = qs_ref[pl.ds(r0, SUB), :]
        gi = gs_ref[pl.ds(r0, SUB), :]
        prods = []
        for j in range(SUB):
            kj = ks_ref[pl.ds(r0 + j, 1), :]
            gj = gs_ref[pl.ds(r0 + j, 1), :]
            m = rin >= j
            prods.append(jnp.where(m, qi * kj * jnp.exp(jnp.where(m, gi - gj, 0.0)), 0.0))
        head_sums = _dot_f32_right(jnp.concatenate(prods, axis=0), bd_ref[...])
        oi = jnp.zeros((SUB, HG_W), F32)
        for j in range(SUB):
            oi = oi + head_sums[j * SUB:(j + 1) * SUB] * vs_ref[pl.ds(r0 + j, 1), :]
        oacc_ref[pl.ds(r0, SUB), :] += oi
        return carry

    lax.fori_loop(0, nsub, diag_block, 0)

    o = oacc_ref[...]
    hms = _dot_f32_right(o * o, bd_ref[...]) * (1.0 / HG_DK)
    o_ref[...] = (o * lax.rsqrt(hms + EPS) * onw_ref[...] * _silu(g_ref[...])).astype(BF16)


def _hgrn2(hq, hf, hi, hg, lb, out_norm, ltri, bd256, bsz, length):
    nc = length // CHUNK
    const = lambda b, c: (0, 0)
    rows = lambda b, c: (b * nc + c, 0)
    need = 64 * CHUNK * HG_W * 4
    return pl.pallas_call(
        _hg_body,
        grid=(bsz, nc),
        in_specs=[pl.BlockSpec((CHUNK, HG_W), rows)] * 4 + [
            pl.BlockSpec((1, HG_W), const),
            pl.BlockSpec((1, HG_W), const),
            pl.BlockSpec((CHUNK, CHUNK), const),
            pl.BlockSpec((HG_W, HG_W), const),
        ],
        out_specs=pl.BlockSpec((CHUNK, HG_W), rows),
        out_shape=jax.ShapeDtypeStruct((bsz * length, HG_W), BF16),
        scratch_shapes=[pltpu.VMEM((HG_W, HG_W), F32)] + [pltpu.VMEM((CHUNK, HG_W), F32)] * 5,
        compiler_params=pltpu.CompilerParams(
            dimension_semantics=("parallel", "arbitrary"), vmem_limit_bytes=_vmem_limit(need)),
        name="hgrn2",
    )(hq, hf, hi, hg, lb, out_norm, ltri, bd256)


def _out_mlp_body(osb_ref, ossd_ref, ohg_ref, h_ref, wo_ref, nw_ref, wup_ref, wdn_ref, out_ref):
    h1 = (h_ref[...]
          + _dot(osb_ref[...], wo_ref[0:SB_W, :])
          + _dot(ossd_ref[...], wo_ref[SB_W:SB_W + SSD_W, :])
          + _dot(ohg_ref[...], wo_ref[SB_W + SSD_W:, :]))
    ms = jnp.mean(h1 * h1, axis=-1, keepdims=True)
    hn = (h1 * lax.rsqrt(ms + EPS) * nw_ref[...]).astype(BF16)
    mlp = jnp.zeros_like(h1)
    for c in range(wup_ref.shape[1] // FF_BLOCK):
        u = _dot(hn, wup_ref[:, c * FF_BLOCK:(c + 1) * FF_BLOCK])
        act = jnp.square(jnp.maximum(u, 0.0)).astype(BF16)
        mlp = mlp + _dot(act, wdn_ref[c * FF_BLOCK:(c + 1) * FF_BLOCK, :])
    out_ref[...] = h1 + mlp


def _out_mlp(o_sb, o_ssd, o_hg, h, w_out, norm_w, w_up, w_down):
    rows, d = h.shape
    d_ff = w_up.shape[1]
    tm = _row_tile(rows, 512)
    const = lambda i: (0, 0)
    tile = lambda i: (i, 0)
    weights = (w_out.shape[0] * d + 2 * d * d_ff) * 2
    need = 2 * weights + 2 * tm * (2 * d * 4 + (SB_W + SSD_W + HG_W) * 2) + 6 * tm * FF_BLOCK * 4
    return pl.pallas_call(
        _out_mlp_body,
        grid=(rows // tm,),
        in_specs=[
            pl.BlockSpec((tm, SB_W), tile),
            pl.BlockSpec((tm, SSD_W), tile),
            pl.BlockSpec((tm, HG_W), tile),
            pl.BlockSpec((tm, d), tile),
            pl.BlockSpec(w_out.shape, const),
            pl.BlockSpec((1, d), const),
            pl.BlockSpec(w_up.shape, const),
            pl.BlockSpec(w_down.shape, const),
        ],
        out_specs=pl.BlockSpec((tm, d), tile),
        out_shape=jax.ShapeDtypeStruct((rows, d), F32),
        compiler_params=pltpu.CompilerParams(
            dimension_semantics=("parallel",), vmem_limit_bytes=_vmem_limit(need)),
        name="out_mlp",
    )(o_sb, o_ssd, o_hg, h, w_out, norm_w, w_up, w_down)


def _block_diag_ones(n, block):
    idx = np.arange(n) // block
    return jnp.asarray(idx[:, None] == idx[None, :], BF16)


def _constants():
    t = np.arange(CHUNK)
    ltri = jnp.asarray(t[None, :] <= t[:, None], BF16)
    later = (t[:, None] > t[None, :])
    usum = jnp.asarray(np.concatenate([later, np.ones((CHUNK, CHUNK), bool)], axis=1), BF16)
    eexp = np.zeros((DT_W, SSD_W), bool)
    for h in range(SSD_HEADS):
        eexp[h, h * HEAD_DIM:(h + 1) * HEAD_DIM] = True
    return ltri, usum, jnp.asarray(eexp, BF16)


def _pack_w_in(w):
    d = w.shape[0]
    dt_lo = 3 * SB_W + 2 * SSD_W + 2 * SSD_BC_W
    dt_hi = dt_lo + SSD_HEADS
    return jnp.concatenate(
        [w[:, :dt_lo], w[:, dt_lo:dt_hi], jnp.zeros((d, DT_W - SSD_HEADS), w.dtype), w[:, dt_hi:]],
        axis=1).astype(BF16)


def _pad_lanes(v, width):
    return jnp.pad(v.astype(F32), (0, width - v.shape[0]))[None, :]


def kernel(x, meta_tokens, hg_lb_logits, norm_mix_w, w_in, sb_q_norm, sb_k_norm, sb_out_norm, ssd_conv_w,
           ssd_conv_b, ssd_dt_bias, ssd_A_log, ssd_D, ssd_norm_w, hg_out_norm, w_out, norm_mlp_w, w_up, w_down):
    bsz, seq, d = x.shape
    depth = w_in.shape[0]
    length = seq + CHUNK
    lead = jnp.concatenate([jnp.zeros((PAD, d), x.dtype), meta_tokens.astype(x.dtype)], axis=0)
    h = jnp.concatenate([jnp.broadcast_to(lead[None], (bsz, CHUNK, d)), x], axis=1).reshape(bsz * length, d)

    probs = jax.nn.softmax(hg_lb_logits.astype(F32), axis=0)
    lbs = jnp.concatenate([jnp.zeros_like(probs[0:1]), jnp.cumsum(probs, axis=0)[:-1]], axis=0)

    ltri, usum, eexp = _constants()
    bd256 = _block_diag_ones(SB_W, HEAD_DIM)
    bd128 = _block_diag_ones(LANES, HEAD_DIM)

    for l in range(depth):
        q, k, v, z, xbc, dt, hq, hf, hi, hg = _in_proj(
            h, norm_mix_w[l][None, :], _pack_w_in(w_in[l]),
            jnp.tile(sb_q_norm[l], SB_HEADS)[None, :], jnp.tile(sb_k_norm[l], SB_HEADS)[None, :], bd256)
        o_sb = _sb_attn(q, k, v, sb_out_norm[l].reshape(1, SB_W), usum, bd128, bsz, length)
        o_ssd = _ssd(z, xbc, dt, ssd_conv_w[l], ssd_conv_b[l][None, :], _pad_lanes(ssd_dt_bias[l], DT_W),
                     _pad_lanes(ssd_A_log[l], DT_W), jnp.repeat(ssd_D[l].astype(F32), HEAD_DIM)[None, :],
                     ssd_norm_w[l].reshape(1, SSD_W), ltri, eexp, bsz, length)
        o_hg = _hgrn2(hq, hf, hi, hg, lbs[l][None, :], hg_out_norm[l].reshape(1, HG_W), ltri, bd256, bsz, length)
        h = _out_mlp(o_sb, o_ssd, o_hg, h, w_out[l].astype(BF16), norm_mlp_w[l][None, :],
                     w_up[l].astype(BF16), w_down[l].astype(BF16))
    return h.reshape(bsz, length, d)[:, CHUNK:]
```

```python
import functools

import numpy as np
import jax
import jax.numpy as jnp
from jax import lax
from jax.experimental import pallas as pl
from jax.experimental.pallas import tpu as pltpu

F32 = jnp.float32
BF16 = jnp.bfloat16

N_META = 16
CHUNK = 128
PAD = CHUNK - N_META
HEAD_DIM = 64
SB_HEADS = 4
SB_W = SB_HEADS * HEAD_DIM
SSD_HEADS = 8
SSD_W = SSD_HEADS * HEAD_DIM
SSD_GROUPS = 2
SSD_STATE = 128
SSD_CONV = 4
SSD_BC_W = SSD_GROUPS * SSD_STATE
SSD_CONV_DIM = SSD_W + 2 * SSD_BC_W
HG_HEADS = 4
HG_DK = 64
HG_W = HG_HEADS * HG_DK
EPS = 1e-6
TINY = 1e-30
SUB = 16
SB_BLOCK = 3 * CHUNK
LANES = 128
DT_W = LANES
FF_BLOCK = 1024
VMEM_CAP_V7X = 64 * 1024 * 1024

_C_Q, _C_K, _C_V = 0, SB_W, 2 * SB_W
_C_Z = 3 * SB_W
_C_XBC = _C_Z + SSD_W
_C_DT = _C_XBC + SSD_CONV_DIM
_C_HQ = _C_DT + DT_W
_C_HF = _C_HQ + HG_W
_C_HI = _C_HF + HG_W
_C_HG = _C_HI + HG_W
D_IN_PACKED = _C_HG + HG_W


def _vmem_limit(need_bytes):
    return int(min(max(need_bytes, 32 * 1024 * 1024), VMEM_CAP_V7X - 6 * 1024 * 1024))


def _row_tile(rows, target):
    t = min(target, rows)
    while rows % t:
        t -= CHUNK
    return t


def _dot(a, b):
    return jnp.dot(a, b, preferred_element_type=F32)


def _dot_nt(a, b):
    return lax.dot_general(a, b, (((1,), (1,)), ((), ())), preferred_element_type=F32)


def _dot_tn(a, b):
    return lax.dot_general(a, b, (((0,), (0,)), ((), ())), preferred_element_type=F32)


def _split2(x):
    hi = x.astype(BF16)
    lo = (x - hi.astype(F32)).astype(BF16)
    return hi, lo


def _split3(x):
    hi = x.astype(BF16)
    r = x - hi.astype(F32)
    mid = r.astype(BF16)
    lo = (r - mid.astype(F32)).astype(BF16)
    return hi, mid, lo


def _dot_f32_right(x, m):
    hi, lo = _split2(x)
    return _dot(hi, m) + _dot(lo, m)


def _dot_f32_left3(m, x):
    hi, mid, lo = _split3(x)
    return _dot(m, hi) + _dot(m, mid) + _dot(m, lo)


def _softplus(x):
    return jnp.maximum(x, 0.0) + jnp.log1p(jnp.exp(-jnp.abs(x)))


def _sigmoid(x):
    return 1.0 / (1.0 + jnp.exp(-x))


def _silu(x):
    return x * _sigmoid(x)


def _in_proj_body(h_ref, nw_ref, w_ref, qn_ref, kn_ref, bd_ref,
                  q_ref, k_ref, v_ref, z_ref, xbc_ref, dt_ref, hq_ref, hf_ref, hi_ref, hg_ref):
    x = h_ref[...]
    ms = jnp.mean(x * x, axis=-1, keepdims=True)
    hn = (x * lax.rsqrt(ms + EPS) * nw_ref[...]).astype(BF16)

    def seg(lo, width):
        return _dot(hn, w_ref[:, lo:lo + width])

    def head_norm(t, w):
        hms = _dot_f32_right(t * t, bd_ref[...]) * (1.0 / HEAD_DIM)
        return t * lax.rsqrt(hms + EPS) * w

    q = head_norm(seg(_C_Q, SB_W), qn_ref[...])
    q_ref[...] = (q * (HEAD_DIM ** -0.5)).astype(BF16)
    k_ref[...] = head_norm(seg(_C_K, SB_W), kn_ref[...]).astype(BF16)
    v_ref[...] = seg(_C_V, SB_W).astype(BF16)
    z_ref[...] = seg(_C_Z, SSD_W)
    xbc_ref[...] = seg(_C_XBC, SSD_CONV_DIM)
    dt_ref[...] = seg(_C_DT, DT_W)
    hq_ref[...] = seg(_C_HQ, HG_W)
    hf_ref[...] = seg(_C_HF, HG_W)
    hi_ref[...] = seg(_C_HI, HG_W)
    hg_ref[...] = seg(_C_HG, HG_W)


def _in_proj(h, norm_w, w_packed, qn, kn, bd256):
    rows, d = h.shape
    tm = _row_tile(rows, 512)
    widths = (SB_W, SB_W, SB_W, SSD_W, SSD_CONV_DIM, DT_W, HG_W, HG_W, HG_W, HG_W)
    dtypes = (BF16, BF16, BF16, F32, F32, F32, F32, F32, F32, F32)
    const = lambda i: (0, 0)
    out_bytes = sum(w * jnp.dtype(t).itemsize for w, t in zip(widths, dtypes)) * tm
    need = 2 * (tm * d * 4 + d * D_IN_PACKED * 2 + out_bytes) + tm * SSD_CONV_DIM * 4 * 4
    return pl.pallas_call(
        _in_proj_body,
        grid=(rows // tm,),
        in_specs=[
            pl.BlockSpec((tm, d), lambda i: (i, 0)),
            pl.BlockSpec((1, d), const),
            pl.BlockSpec((d, D_IN_PACKED), const),
            pl.BlockSpec((1, SB_W), const),
            pl.BlockSpec((1, SB_W), const),
            pl.BlockSpec((SB_W, SB_W), const),
        ],
        out_specs=[pl.BlockSpec((tm, w), lambda i: (i, 0)) for w in widths],
        out_shape=[jax.ShapeDtypeStruct((rows, w), t) for w, t in zip(widths, dtypes)],
        compiler_params=pltpu.CompilerParams(
            dimension_semantics=("parallel",), vmem_limit_bytes=_vmem_limit(need)),
        name="in_proj",
    )(h, norm_w, w_packed, qn, kn, bd256)


def _sb_body(q_ref, k_ref, v_ref, onw_ref, usum_ref, bd_ref, o_ref, acc_ref, carry_ref):
    ib = pl.program_id(2)
    nsub = SB_BLOCK // CHUNK
    lane = lax.broadcasted_iota(jnp.int32, (SB_BLOCK, LANES), 1)
    q = q_ref[...]
    zero = jnp.zeros_like(q)
    q2 = jnp.concatenate([jnp.where(lane < HEAD_DIM, q, zero), jnp.where(lane >= HEAD_DIM, q, zero)], axis=0)
    acc_ref[...] = jnp.zeros_like(acc_ref)
    carry_ref[...] = jnp.zeros_like(carry_ref)

    def tile(jb, masked):
        off = pl.multiple_of(jb * SB_BLOCK, SB_BLOCK)
        kb = k_ref[pl.ds(off, SB_BLOCK), :]
        vb = v_ref[pl.ds(off, SB_BLOCK), :]
        z = _dot_nt(q2, kb)
        sp = jnp.maximum(z, 0.0) + jnp.log(1.0 + jnp.exp(-jnp.abs(z)))
        drop = sp
        if masked:
            row = lax.broadcasted_iota(jnp.int32, (SB_BLOCK, SB_BLOCK), 0)
            col = lax.broadcasted_iota(jnp.int32, (SB_BLOCK, SB_BLOCK), 1)
            key_pos = jb * SB_BLOCK + col
            mask1 = (key_pos < ib * SB_BLOCK + row) & (key_pos >= PAD)
            mask = jnp.concatenate([mask1, mask1], axis=0)
            drop = jnp.where(mask, sp, 0.0)
        hi, lo = _split2(drop)
        carry = carry_ref[...]
        ws = [None] * nsub
        for m in reversed(range(nsub)):
            cols = slice(m * CHUNK, (m + 1) * CHUNK)
            sums = _dot(jnp.concatenate([hi[:, cols], lo[:, cols]], axis=1), usum_ref[...])
            log_w = (z[:, cols] - sp[:, cols]) - sums[:, :CHUNK] - carry
            carry = carry + sums[:, CHUNK:]
            if masked:
                w = jnp.where(mask[:, cols], jnp.exp(jnp.where(mask[:, cols], log_w, 0.0)), 0.0)
            else:
                w = jnp.exp(log_w)
            ws[m] = w.astype(BF16)
        carry_ref[...] = carry
        acc_ref[...] += _dot(jnp.concatenate(ws, axis=1), vb)

    tile(ib, True)

    def body(jj, carry):
        tile(ib - 1 - jj, False)
        return carry

    lax.fori_loop(0, jnp.maximum(ib - 1, 0), body, 0)

    @pl.when(ib > 0)
    def _():
        tile(0, True)

    o = jnp.where(lane < HEAD_DIM, acc_ref[0:SB_BLOCK, :], acc_ref[SB_BLOCK:, :])
    hms = _dot_f32_right(o * o, bd_ref[...]) * (1.0 / HEAD_DIM)
    o_ref[...] = (o * lax.rsqrt(hms + EPS) * onw_ref[...]).astype(BF16)


def _sb_attn(q, k, v, out_norm, usum, bd128, bsz, length):
    assert length % SB_BLOCK == 0
    nb = length // SB_BLOCK
    const = lambda b, p, i: (0, 0)
    need = 2 * (2 * length * LANES * 2) + 24 * 2 * SB_BLOCK * SB_BLOCK * 4
    return pl.pallas_call(
        _sb_body,
        grid=(bsz, SB_W // LANES, nb),
        in_specs=[
            pl.BlockSpec((SB_BLOCK, LANES), lambda b, p, i: (b * nb + i, p)),
            pl.BlockSpec((length, LANES), lambda b, p, i: (b, p)),
            pl.BlockSpec((length, LANES), lambda b, p, i: (b, p)),
            pl.BlockSpec((1, LANES), lambda b, p, i: (0, p)),
            pl.BlockSpec((2 * CHUNK, 2 * CHUNK), const),
            pl.BlockSpec((LANES, LANES), const),
        ],
        out_specs=pl.BlockSpec((SB_BLOCK, LANES), lambda b, p, i: (b * nb + i, p)),
        out_shape=jax.ShapeDtypeStruct((bsz * length, SB_W), BF16),
        scratch_shapes=[pltpu.VMEM((2 * SB_BLOCK, LANES), F32), pltpu.VMEM((2 * SB_BLOCK, CHUNK), F32)],
        compiler_params=pltpu.CompilerParams(
            dimension_semantics=("parallel", "parallel", "arbitrary"), vmem_limit_bytes=_vmem_limit(need)),
        name="sb_attn",
    )(q, k, v, out_norm, usum, bd128)


def _ssd_body(z_ref, xbc_ref, dt_ref, cw_ref, cb_ref, dtb_ref, alog_ref, dexp_ref, nw_ref, ltri_ref, eexp_ref,
              o_ref, buf_ref, st_ref):
    c = pl.program_id(1)
    heads_per_group = SSD_HEADS // SSD_GROUPS
    group_w = SSD_W // SSD_GROUPS
    tail = SSD_CONV - 1

    @pl.when(c == 0)
    def _():
        st_ref[...] = jnp.zeros_like(st_ref)
        buf_ref[0:8, :] = jnp.zeros((8, SSD_CONV_DIM), F32)

    rowi = lax.broadcasted_iota(jnp.int32, (CHUNK, 1), 0)
    valid = (c * CHUNK + rowi) >= PAD
    buf_ref[8:8 + CHUNK, :] = jnp.where(valid, xbc_ref[...], 0.0)
    conv = cb_ref[...]
    for i in range(SSD_CONV):
        conv = conv + cw_ref[i:i + 1, :] * buf_ref[8 - tail + i:8 - tail + i + CHUNK, :]
    buf_ref[0:8, :] = buf_ref[CHUNK:CHUNK + 8, :]
    act = _silu(conv)
    xs = act[:, :SSD_W]
    bm = act[:, SSD_W:SSD_W + SSD_BC_W].astype(BF16)
    cm = act[:, SSD_W + SSD_BC_W:].astype(BF16)

    dt = jnp.where(valid, _softplus(dt_ref[...] + dtb_ref[...]), 0.0)
    a = dt * (-jnp.exp(alog_ref[...]))
    acum = _dot_f32_left3(ltri_ref[...], a)
    acum_t = acum.T
    a_last = acum[CHUNK - 1:CHUNK, :]
    per_head = jnp.concatenate([dt, jnp.exp(acum), jnp.exp(a_last - acum)], axis=0)
    expanded = _dot_f32_right(per_head, eexp_ref[...])
    dt_e = expanded[:CHUNK]
    decay_in_e = expanded[CHUNK:2 * CHUNK]
    decay_out_e = expanded[2 * CHUNK:]
    xdt = xs * dt_e
    xdt_b = xdt.astype(BF16)
    xw_b = (xdt * decay_out_e).astype(BF16)

    row = lax.broadcasted_iota(jnp.int32, (CHUNK, CHUNK), 0)
    col = lax.broadcasted_iota(jnp.int32, (CHUNK, CHUNK), 1)
    causal = row >= col
    lane = lax.broadcasted_iota(jnp.int32, (CHUNK, LANES), 1)
    upper_half = lane >= HEAD_DIM

    ys = []
    for g in range(SSD_GROUPS):
        cg = cm[:, g * SSD_STATE:(g + 1) * SSD_STATE]
        bg = bm[:, g * SSD_STATE:(g + 1) * SSD_STATE]
        gcols = slice(g * group_w, (g + 1) * group_w)
        cb = _dot_nt(cg, bg)
        st = st_ref[g]
        y_off = _dot(cg, st.astype(BF16)) * decay_in_e[:, gcols]
        pairs = []
        for pr in range(heads_per_group // 2):
            xp = xdt_b[:, g * group_w + pr * LANES:g * group_w + (pr + 1) * LANES]
            acc = None
            for hh in range(2):
                h = g * heads_per_group + pr * 2 + hh
                seg = acum[:, h:h + 1] - acum_t[h:h + 1, :]
                decay = jnp.where(causal, jnp.exp(jnp.where(causal, seg, 0.0)), 0.0)
                m = (cb * decay).astype(BF16)
                keep = upper_half if hh else jnp.logical_not(upper_half)
                t = _dot(m, jnp.where(keep, xp, jnp.zeros_like(xp)))
                acc = t if acc is None else acc + t
            pairs.append(acc)
        ys.append(jnp.concatenate(pairs, axis=1) + y_off)
        st_ref[g] = st * decay_in_e[CHUNK - 1:CHUNK, gcols] + _dot_tn(bg, xw_b[:, gcols])

    y = jnp.concatenate(ys, axis=1) + xs * dexp_ref[...]
    y = y * _silu(z_ref[...])
    outs = []
    for g in range(SSD_GROUPS):
        yg = y[:, g * group_w:(g + 1) * group_w]
        gms = jnp.mean(yg * yg, axis=-1, keepdims=True)
        outs.append(yg * lax.rsqrt(gms + EPS) * nw_ref[:, g * group_w:(g + 1) * group_w])
    o_ref[...] = jnp.concatenate(outs, axis=1).astype(BF16)


def _ssd(z, xbc, dt, conv_w, conv_b, dt_bias, a_log, d_exp, norm_w, ltri, eexp, bsz, length):
    nc = length // CHUNK
    const = lambda b, c: (0, 0)
    rows = lambda b, c: (b * nc + c, 0)
    group_w = SSD_W // SSD_GROUPS
    need = 2 * CHUNK * (SSD_W + SSD_CONV_DIM + DT_W) * 4 * 2 + 64 * CHUNK * SSD_CONV_DIM * 4
    return pl.pallas_call(
        _ssd_body,
        grid=(bsz, nc),
        in_specs=[
            pl.BlockSpec((CHUNK, SSD_W), rows),
            pl.BlockSpec((CHUNK, SSD_CONV_DIM), rows),
            pl.BlockSpec((CHUNK, DT_W), rows),
            pl.BlockSpec((SSD_CONV, SSD_CONV_DIM), const),
            pl.BlockSpec((1, SSD_CONV_DIM), const),
            pl.BlockSpec((1, DT_W), const),
            pl.BlockSpec((1, DT_W), const),
            pl.BlockSpec((1, SSD_W), const),
            pl.BlockSpec((1, SSD_W), const),
            pl.BlockSpec((CHUNK, CHUNK), const),
            pl.BlockSpec((DT_W, SSD_W), const),
        ],
        out_specs=pl.BlockSpec((CHUNK, SSD_W), rows),
        out_shape=jax.ShapeDtypeStruct((bsz * length, SSD_W), BF16),
        scratch_shapes=[pltpu.VMEM((CHUNK + 8, SSD_CONV_DIM), F32),
                        pltpu.VMEM((SSD_GROUPS, SSD_STATE, group_w), F32)],
        compiler_params=pltpu.CompilerParams(
            dimension_semantics=("parallel", "arbitrary"), vmem_limit_bytes=_vmem_limit(need)),
        name="ssd",
    )(z, xbc, dt, conv_w, conv_b, dt_bias, a_log, d_exp, norm_w, ltri, eexp)


def _hg_body(q_ref, f_ref, i_ref, g_ref, lb_ref, onw_ref, ltri_ref, bd_ref,
             o_ref, st_ref, qs_ref, ks_ref, vs_ref, gs_ref, oacc_ref):
    c = pl.program_id(1)
    nsub = CHUNK // SUB

    @pl.when(c == 0)
    def _():
        st_ref[...] = jnp.zeros_like(st_ref)

    rowi = lax.broadcasted_iota(jnp.int32, (CHUNK, 1), 0)
    valid = (c * CHUNK + rowi) >= PAD
    lb = lb_ref[...]
    fl = f_ref[...]
    f = lb + (1.0 - lb) * _sigmoid(fl)
    log_f = jnp.where(valid, jnp.log(jnp.maximum(f, TINY)), 0.0)
    k = jnp.where(valid, (1.0 - lb) * _sigmoid(-fl), 0.0)
    v = jnp.where(valid, i_ref[...], 0.0)
    q = _silu(q_ref[...])
    gc = _dot_f32_left3(ltri_ref[...], log_f)
    qs_ref[...] = q
    ks_ref[...] = k
    vs_ref[...] = v
    gs_ref[...] = gc
    g_last = gc[CHUNK - 1:CHUNK, :]

    st = st_ref[...]
    oacc_ref[...] = _dot_nt((q * jnp.exp(gc)).astype(BF16), st.astype(BF16))
    k_end = (k * jnp.exp(g_last - gc)).astype(BF16)
    srow = lax.broadcasted_iota(jnp.int32, (HG_W, HG_W), 0) // HG_DK
    scol = lax.broadcasted_iota(jnp.int32, (HG_W, HG_W), 1) // HG_DK
    st_ref[...] = st * jnp.exp(g_last) + jnp.where(srow == scol, _dot_tn(v.astype(BF16), k_end), 0.0)

    erow = lax.broadcasted_iota(jnp.int32, (HG_HEADS * SUB, HG_W), 0) // SUB
    ecol = lax.broadcasted_iota(jnp.int32, (HG_HEADS * SUB, HG_W), 1) // HG_DK
    same_head = erow == ecol
    for jb in range(nsub - 1):
        r0, r1 = jb * SUB, (jb + 1) * SUB
        g_end = gc[r1 - 1:r1, :]
        ke = k[r0:r1] * jnp.exp(g_end - gc[r0:r1])
        ke4 = jnp.where(same_head, jnp.concatenate([ke] * HG_HEADS, axis=0), 0.0).astype(BF16)
        v4 = jnp.where(same_head, jnp.concatenate([v[r0:r1]] * HG_HEADS, axis=0), 0.0).astype(BF16)
        qp = (q[r1:] * jnp.exp(gc[r1:] - g_end)).astype(BF16)
        scores = _dot_nt(qp, ke4)
        oacc_ref[r1:, :] += _dot(scores.astype(BF16), v4)

    rin = lax.broadcasted_iota(jnp.int32, (SUB, 1), 0)

    def diag_block(ib, carry):
        r0 = pl.multiple_of(ib * SUB, SUB)
        qi = qs_ref[pl.ds(r0, SUB), :]
        gi = gs_ref[pl.ds(r0, SUB), :]
        prods = []
        for j in range(SUB):
            kj = ks_ref[pl.ds(r0 + j, 1), :]
            gj = gs_ref[pl.ds(r0 + j, 1), :]
            m = rin >= j
            prods.append(jnp.where(m, qi * kj * jnp.exp(jnp.where(m, gi - gj, 0.0)), 0.0))
        head_sums = _dot_f32_right(jnp.concatenate(prods, axis=0), bd_ref[...])
        oi = jnp.zeros((SUB, HG_W), F32)
        for j in range(SUB):
            oi = oi + head_sums[j * SUB:(j + 1) * SUB] * vs_ref[pl.ds(r0 + j, 1), :]
        oacc_ref[pl.ds(r0, SUB), :] += oi
        return carry

    lax.fori_loop(0, nsub, diag_block, 0)

    o = oacc_ref[...]
    hms = _dot_f32_right(o * o, bd_ref[...]) * (1.0 / HG_DK)
    o_ref[...] = (o * lax.rsqrt(hms + EPS) * onw_ref[...] * _silu(g_ref[...])).astype(BF16)


def _hgrn2(hq, hf, hi, hg, lb, out_norm, ltri, bd256, bsz, length):
    nc = length // CHUNK
    const = lambda b, c: (0, 0)
    rows = lambda b, c: (b * nc + c, 0)
    need = 64 * CHUNK * HG_W * 4
    return pl.pallas_call(
        _hg_body,
        grid=(bsz, nc),
        in_specs=[pl.BlockSpec((CHUNK, HG_W), rows)] * 4 + [
            pl.BlockSpec((1, HG_W), const),
            pl.BlockSpec((1, HG_W), const),
            pl.BlockSpec((CHUNK, CHUNK), const),
            pl.BlockSpec((HG_W, HG_W), const),
        ],
        out_specs=pl.BlockSpec((CHUNK, HG_W), rows),
        out_shape=jax.ShapeDtypeStruct((bsz * length, HG_W), BF16),
        scratch_shapes=[pltpu.VMEM((HG_W, HG_W), F32)] + [pltpu.VMEM((CHUNK, HG_W), F32)] * 5,
        compiler_params=pltpu.CompilerParams(
            dimension_semantics=("parallel", "arbitrary"), vmem_limit_bytes=_vmem_limit(need)),
        name="hgrn2",
    )(hq, hf, hi, hg, lb, out_norm, ltri, bd256)


def _out_mlp_body(osb_ref, ossd_ref, ohg_ref, h_ref, wo_ref, nw_ref, wup_ref, wdn_ref, out_ref):
    h1 = (h_ref[...]
          + _dot(osb_ref[...], wo_ref[0:SB_W, :])
          + _dot(ossd_ref[...], wo_ref[SB_W:SB_W + SSD_W, :])
          + _dot(ohg_ref[...], wo_ref[SB_W + SSD_W:, :]))
    ms = jnp.mean(h1 * h1, axis=-1, keepdims=True)
    hn = (h1 * lax.rsqrt(ms + EPS) * nw_ref[...]).astype(BF16)
    mlp = jnp.zeros_like(h1)
    for c in range(wup_ref.shape[1] // FF_BLOCK):
        u = _dot(hn, wup_ref[:, c * FF_BLOCK:(c + 1) * FF_BLOCK])
        act = jnp.square(jnp.maximum(u, 0.0)).astype(BF16)
        mlp = mlp + _dot(act, wdn_ref[c * FF_BLOCK:(c + 1) * FF_BLOCK, :])
    out_ref[...] = h1 + mlp


def _out_mlp(o_sb, o_ssd, o_hg, h, w_out, norm_w, w_up, w_down):
    rows, d = h.shape
    d_ff = w_up.shape[1]
    tm = _row_tile(rows, 512)
    const = lambda i: (0, 0)
    tile = lambda i: (i, 0)
    weights = (w_out.shape[0] * d + 2 * d * d_ff) * 2
    need = 2 * weights + 2 * tm * (2 * d * 4 + (SB_W + SSD_W + HG_W) * 2) + 6 * tm * FF_BLOCK * 4
    return pl.pallas_call(
        _out_mlp_body,
        grid=(rows // tm,),
        in_specs=[
            pl.BlockSpec((tm, SB_W), tile),
            pl.BlockSpec((tm, SSD_W), tile),
            pl.BlockSpec((tm, HG_W), tile),
            pl.BlockSpec((tm, d), tile),
            pl.BlockSpec(w_out.shape, const),
            pl.BlockSpec((1, d), const),
            pl.BlockSpec(w_up.shape, const),
            pl.BlockSpec(w_down.shape, const),
        ],
        out_specs=pl.BlockSpec((tm, d), tile),
        out_shape=jax.ShapeDtypeStruct((rows, d), F32),
        compiler_params=pltpu.CompilerParams(
            dimension_semantics=("parallel",), vmem_limit_bytes=_vmem_limit(need)),
        name="out_mlp",
    )(o_sb, o_ssd, o_hg, h, w_out, norm_w, w_up, w_down)


def _block_diag_ones(n, block):
    idx = np.arange(n) // block
    return jnp.asarray(idx[:, None] == idx[None, :], BF16)


def _constants():
    t = np.arange(CHUNK)
    ltri = jnp.asarray(t[None, :] <= t[:, None], BF16)
    later = (t[:, None] > t[None, :])
    usum = np.concatenate([later, np.ones((CHUNK, CHUNK), bool)], axis=1)
    usum = jnp.asarray(np.concatenate([usum, usum], axis=0), BF16)
    eexp = np.zeros((DT_W, SSD_W), bool)
    for h in range(SSD_HEADS):
        eexp[h, h * HEAD_DIM:(h + 1) * HEAD_DIM] = True
    return ltri, usum, jnp.asarray(eexp, BF16)


def _pack_w_in(w):
    d = w.shape[0]
    dt_lo = 3 * SB_W + 2 * SSD_W + 2 * SSD_BC_W
    dt_hi = dt_lo + SSD_HEADS
    return jnp.concatenate(
        [w[:, :dt_lo], w[:, dt_lo:dt_hi], jnp.zeros((d, DT_W - SSD_HEADS), w.dtype), w[:, dt_hi:]],
        axis=1).astype(BF16)


def _pad_lanes(v, width):
    return jnp.pad(v.astype(F32), (0, width - v.shape[0]))[None, :]


def kernel(x, meta_tokens, hg_lb_logits, norm_mix_w, w_in, sb_q_norm, sb_k_norm, sb_out_norm, ssd_conv_w,
           ssd_conv_b, ssd_dt_bias, ssd_A_log, ssd_D, ssd_norm_w, hg_out_norm, w_out, norm_mlp_w, w_up, w_down):
    bsz, seq, d = x.shape
    depth = w_in.shape[0]
    length = seq + CHUNK
    lead = jnp.concatenate([jnp.zeros((PAD, d), x.dtype), meta_tokens.astype(x.dtype)], axis=0)
    h = jnp.concatenate([jnp.broadcast_to(lead[None], (bsz, CHUNK, d)), x], axis=1).reshape(bsz * length, d)

    probs = jax.nn.softmax(hg_lb_logits.astype(F32), axis=0)
    lbs = jnp.concatenate([jnp.zeros_like(probs[0:1]), jnp.cumsum(probs, axis=0)[:-1]], axis=0)

    ltri, usum, eexp = _constants()
    bd256 = _block_diag_ones(SB_W, HEAD_DIM)
    bd128 = _block_diag_ones(LANES, HEAD_DIM)

    for l in range(depth):
        q, k, v, z, xbc, dt, hq, hf, hi, hg = _in_proj(
            h, norm_mix_w[l][None, :], _pack_w_in(w_in[l]),
            jnp.tile(sb_q_norm[l], SB_HEADS)[None, :], jnp.tile(sb_k_norm[l], SB_HEADS)[None, :], bd256)
        o_sb = _sb_attn(q, k, v, sb_out_norm[l].reshape(1, SB_W), usum, bd128, bsz, length)
        o_ssd = _ssd(z, xbc, dt, ssd_conv_w[l], ssd_conv_b[l][None, :], _pad_lanes(ssd_dt_bias[l], DT_W),
                     _pad_lanes(ssd_A_log[l], DT_W), jnp.repeat(ssd_D[l].astype(F32), HEAD_DIM)[None, :],
                     ssd_norm_w[l].reshape(1, SSD_W), ltri, eexp, bsz, length)
        o_hg = _hgrn2(hq, hf, hi, hg, lbs[l][None, :], hg_out_norm[l].reshape(1, HG_W), ltri, bd256, bsz, length)
        h = _out_mlp(o_sb, o_ssd, o_hg, h, w_out[l].astype(BF16), norm_mlp_w[l][None, :],
                     w_up[l].astype(BF16), w_down[l].astype(BF16))
    return h.reshape(bsz, length, d)[:, CHUNK:]
```

```python
import functools

import numpy as np
import jax
import jax.numpy as jnp
from jax import lax
from jax.experimental import pallas as pl
from jax.experimental.pallas import tpu as pltpu

F32 = jnp.float32
BF16 = jnp.bfloat16

N_META = 16
CHUNK = 128
PAD = CHUNK - N_META
HEAD_DIM = 64
SB_HEADS = 4
SB_W = SB_HEADS * HEAD_DIM
SSD_HEADS = 8
SSD_W = SSD_HEADS * HEAD_DIM
SSD_GROUPS = 2
SSD_STATE = 128
SSD_CONV = 4
SSD_BC_W = SSD_GROUPS * SSD_STATE
SSD_CONV_DIM = SSD_W + 2 * SSD_BC_W
HG_HEADS = 4
HG_DK = 64
HG_W = HG_HEADS * HG_DK
EPS = 1e-6
TINY = 1e-30
LOG2E = 1.4426950408889634
HG_SAFE_SPAN = 60.0
SUB = 16
SB_BLOCK = 3 * CHUNK
LANES = 128
DT_W = LANES
FF_BLOCK = 1024
VMEM_CAP_V7X = 64 * 1024 * 1024

_C_Q, _C_K, _C_V = 0, SB_W, 2 * SB_W
_C_Z = 3 * SB_W
_C_XBC = _C_Z + SSD_W
_C_DT = _C_XBC + SSD_CONV_DIM
_C_HQ = _C_DT + DT_W
_C_HF = _C_HQ + HG_W
_C_HI = _C_HF + HG_W
_C_HG = _C_HI + HG_W
D_IN_PACKED = _C_HG + HG_W


def _vmem_limit(need_bytes):
    return int(min(max(need_bytes, 32 * 1024 * 1024), VMEM_CAP_V7X - 6 * 1024 * 1024))


def _row_tile(rows, target):
    t = min(target, rows)
    while rows % t:
        t -= CHUNK
    return t


def _dot(a, b):
    return jnp.dot(a, b, preferred_element_type=F32)


def _dot_nt(a, b):
    return lax.dot_general(a, b, (((1,), (1,)), ((), ())), preferred_element_type=F32)


def _dot_tn(a, b):
    return lax.dot_general(a, b, (((0,), (0,)), ((), ())), preferred_element_type=F32)


def _split2(x):
    hi = x.astype(BF16)
    lo = (x - hi.astype(F32)).astype(BF16)
    return hi, lo


def _split3(x):
    hi = x.astype(BF16)
    r = x - hi.astype(F32)
    mid = r.astype(BF16)
    lo = (r - mid.astype(F32)).astype(BF16)
    return hi, mid, lo


def _dot_f32_right(x, m):
    hi, lo = _split2(x)
    return _dot(hi, m) + _dot(lo, m)


def _dot_f32_left3(m, x):
    hi, mid, lo = _split3(x)
    return _dot(m, hi) + _dot(m, mid) + _dot(m, lo)


def _softplus(x):
    return jnp.maximum(x, 0.0) + jnp.log(1.0 + jnp.exp(-jnp.abs(x)))


def _sigmoid(x):
    return 1.0 / (1.0 + jnp.exp(-x))


def _silu(x):
    return x * _sigmoid(x)


def _in_proj_body(h_ref, nw_ref, w_ref, qn_ref, kn_ref, bd_ref, dtb_ref, lb_ref,
                  q_ref, k_ref, v_ref, zg_ref, xbc_ref, dt_ref, hq_ref, hlf_ref, hk_ref, hv_ref, hg_ref):
    x = h_ref[...]
    ms = jnp.mean(x * x, axis=-1, keepdims=True)
    hn = (x * lax.rsqrt(ms + EPS) * nw_ref[...]).astype(BF16)

    def seg(lo, width):
        return _dot(hn, w_ref[:, lo:lo + width])

    def head_norm(t, w):
        hms = _dot_f32_right(t * t, bd_ref[...]) * (1.0 / HEAD_DIM)
        return t * lax.rsqrt(hms + EPS) * w

    q = head_norm(seg(_C_Q, SB_W), qn_ref[...])
    q_ref[...] = (q * (HEAD_DIM ** -0.5 * LOG2E)).astype(BF16)
    k_ref[...] = head_norm(seg(_C_K, SB_W), kn_ref[...]).astype(BF16)
    v_ref[...] = seg(_C_V, SB_W).astype(BF16)
    zg_ref[...] = _silu(seg(_C_Z, SSD_W))
    xbc_ref[...] = seg(_C_XBC, SSD_CONV_DIM)
    dt_ref[...] = _softplus(seg(_C_DT, DT_W) + dtb_ref[...])
    hq_ref[...] = _silu(seg(_C_HQ, HG_W))
    lb = lb_ref[...]
    fl = seg(_C_HF, HG_W)
    hlf_ref[...] = jnp.log(jnp.maximum(lb + (1.0 - lb) * _sigmoid(fl), TINY))
    hk_ref[...] = (1.0 - lb) * _sigmoid(-fl)
    hv_ref[...] = seg(_C_HI, HG_W)
    hg_ref[...] = _silu(seg(_C_HG, HG_W))


def _in_proj(h, norm_w, w_packed, qn, kn, bd256, dt_bias, lb):
    rows, d = h.shape
    tm = _row_tile(rows, 512)
    widths = (SB_W, SB_W, SB_W, SSD_W, SSD_CONV_DIM, DT_W, HG_W, HG_W, HG_W, HG_W, HG_W)
    dtypes = (BF16, BF16, BF16, F32, F32, F32, F32, F32, F32, F32, F32)
    const = lambda i: (0, 0)
    out_bytes = sum(w * jnp.dtype(t).itemsize for w, t in zip(widths, dtypes)) * tm
    need = 2 * (tm * d * 4 + d * D_IN_PACKED * 2 + out_bytes) + tm * SSD_CONV_DIM * 4 * 4
    return pl.pallas_call(
        _in_proj_body,
        grid=(rows // tm,),
        in_specs=[
            pl.BlockSpec((tm, d), lambda i: (i, 0)),
            pl.BlockSpec((1, d), const),
            pl.BlockSpec((d, D_IN_PACKED), const),
            pl.BlockSpec((1, SB_W), const),
            pl.BlockSpec((1, SB_W), const),
            pl.BlockSpec((SB_W, SB_W), const),
            pl.BlockSpec((1, DT_W), const),
            pl.BlockSpec((1, HG_W), const),
        ],
        out_specs=[pl.BlockSpec((tm, w), lambda i: (i, 0)) for w in widths],
        out_shape=[jax.ShapeDtypeStruct((rows, w), t) for w, t in zip(widths, dtypes)],
        compiler_params=pltpu.CompilerParams(
            dimension_semantics=("parallel",), vmem_limit_bytes=_vmem_limit(need)),
        name="in_proj",
    )(h, norm_w, w_packed, qn, kn, bd256, dt_bias, lb)


def _sb_body(q_ref, k_ref, v_ref, onw_ref, usum_ref, bd_ref, o_ref, acc_ref, carry_ref):
    ib = pl.program_id(2)
    nsub = SB_BLOCK // CHUNK
    lane = lax.broadcasted_iota(jnp.int32, (SB_BLOCK, LANES), 1)
    q = q_ref[...]
    zero = jnp.zeros_like(q)
    q2 = jnp.concatenate([jnp.where(lane < HEAD_DIM, q, zero), jnp.where(lane >= HEAD_DIM, q, zero)], axis=0)
    acc_ref[...] = jnp.zeros_like(acc_ref)
    carry_ref[...] = jnp.zeros_like(carry_ref)

    def tile(jb, masked):
        off = pl.multiple_of(jb * SB_BLOCK, SB_BLOCK)
        kb = k_ref[pl.ds(off, SB_BLOCK), :]
        vb = v_ref[pl.ds(off, SB_BLOCK), :]
        z = _dot_nt(q2, kb)
        sp = jnp.maximum(z, 0.0) + jnp.log2(1.0 + jnp.exp2(-jnp.abs(z)))
        drop = sp
        if masked:
            row = lax.broadcasted_iota(jnp.int32, (SB_BLOCK, SB_BLOCK), 0)
            col = lax.broadcasted_iota(jnp.int32, (SB_BLOCK, SB_BLOCK), 1)
            key_pos = jb * SB_BLOCK + col
            mask1 = (key_pos < ib * SB_BLOCK + row) & (key_pos >= PAD)
            mask = jnp.concatenate([mask1, mask1], axis=0)
            drop = jnp.where(mask, sp, 0.0)
        drop = drop.astype(BF16)
        carry = carry_ref[...]
        ws = [None] * nsub
        for m in reversed(range(nsub)):
            cols = slice(m * CHUNK, (m + 1) * CHUNK)
            sums = _dot(drop[:, cols], usum_ref[...])
            log_w = (z[:, cols] - sp[:, cols]) - sums[:, :CHUNK] - carry
            carry = carry + sums[:, CHUNK:]
            if masked:
                w = jnp.where(mask[:, cols], jnp.exp2(jnp.where(mask[:, cols], log_w, 0.0)), 0.0)
            else:
                w = jnp.exp2(log_w)
            ws[m] = w.astype(BF16)
        carry_ref[...] = carry
        acc_ref[...] += _dot(jnp.concatenate(ws, axis=1), vb)

    tile(ib, True)

    def body(jj, carry):
        tile(ib - 1 - jj, False)
        return carry

    lax.fori_loop(0, jnp.maximum(ib - 1, 0), body, 0)

    @pl.when(ib > 0)
    def _():
        tile(0, True)

    o = jnp.where(lane < HEAD_DIM, acc_ref[0:SB_BLOCK, :], acc_ref[SB_BLOCK:, :])
    hms = _dot_f32_right(o * o, bd_ref[...]) * (1.0 / HEAD_DIM)
    o_ref[...] = (o * lax.rsqrt(hms + EPS) * onw_ref[...]).astype(BF16)


def _sb_attn(q, k, v, out_norm, usum, bd128, bsz, length):
    assert length % SB_BLOCK == 0
    nb = length // SB_BLOCK
    const = lambda b, p, i: (0, 0)
    need = 2 * (2 * length * LANES * 2) + 24 * 2 * SB_BLOCK * SB_BLOCK * 4
    return pl.pallas_call(
        _sb_body,
        grid=(bsz, SB_W // LANES, nb),
        in_specs=[
            pl.BlockSpec((SB_BLOCK, LANES), lambda b, p, i: (b * nb + i, p)),
            pl.BlockSpec((length, LANES), lambda b, p, i: (b, p)),
            pl.BlockSpec((length, LANES), lambda b, p, i: (b, p)),
            pl.BlockSpec((1, LANES), lambda b, p, i: (0, p)),
            pl.BlockSpec((CHUNK, 2 * CHUNK), const),
            pl.BlockSpec((LANES, LANES), const),
        ],
        out_specs=pl.BlockSpec((SB_BLOCK, LANES), lambda b, p, i: (b * nb + i, p)),
        out_shape=jax.ShapeDtypeStruct((bsz * length, SB_W), BF16),
        scratch_shapes=[pltpu.VMEM((2 * SB_BLOCK, LANES), F32), pltpu.VMEM((2 * SB_BLOCK, CHUNK), F32)],
        compiler_params=pltpu.CompilerParams(
            dimension_semantics=("parallel", "parallel", "arbitrary"), vmem_limit_bytes=_vmem_limit(need)),
        name="sb_attn",
    )(q, k, v, out_norm, usum, bd128)


def _ssd_body(zg_ref, xbc_ref, dt_ref, cw_ref, cb_ref, alog_ref, dexp_ref, nw_ref, ltri_ref, eexp_ref, shift_ref,
              o_ref, tail_ref, st_ref):
    c = pl.program_id(1)
    heads_per_group = SSD_HEADS // SSD_GROUPS
    group_w = SSD_W // SSD_GROUPS
    taps = SSD_CONV - 1

    @pl.when(c == 0)
    def _():
        st_ref[...] = jnp.zeros_like(st_ref)
        tail_ref[...] = jnp.zeros_like(tail_ref)

    rowi = lax.broadcasted_iota(jnp.int32, (CHUNK, 1), 0)
    valid = (c * CHUNK + rowi) >= PAD
    u = jnp.where(valid, xbc_ref[...], 0.0)
    u_hi, u_lo = _split2(u)
    shifted = _dot(shift_ref[...], jnp.concatenate([u_hi, u_lo], axis=0))
    conv = cb_ref[...] + cw_ref[taps:taps + 1, :] * u
    for kk in range(1, taps + 1):
        conv = conv + cw_ref[taps - kk:taps - kk + 1, :] * shifted[(kk - 1) * CHUNK:kk * CHUNK]
    tail = tail_ref[...]
    row8 = lax.broadcasted_iota(jnp.int32, (8, 1), 0)
    head_fix = jnp.zeros((8, SSD_CONV_DIM), F32)
    for kk in range(1, taps + 1):
        head_fix = head_fix + cw_ref[taps - kk:taps - kk + 1, :] * jnp.where(row8 < kk, pltpu.roll(tail, kk, 0), 0.0)
    conv = jnp.concatenate([conv[:8] + head_fix, conv[8:]], axis=0)
    tail_ref[...] = u[CHUNK - 8:]
    act = _silu(conv)
    xs = act[:, :SSD_W]
    bm = act[:, SSD_W:SSD_W + SSD_BC_W].astype(BF16)
    cm = act[:, SSD_W + SSD_BC_W:].astype(BF16)

    dt = jnp.where(valid, dt_ref[...], 0.0)
    a = dt * (-jnp.exp(alog_ref[...]))
    acum = _dot_f32_left3(ltri_ref[...], a)
    acum_t = acum.T
    a_last = acum[CHUNK - 1:CHUNK, :]
    per_head = jnp.concatenate([dt, jnp.exp(acum), jnp.exp(a_last - acum)], axis=0)
    expanded = _dot_f32_right(per_head, eexp_ref[...])
    dt_e = expanded[:CHUNK]
    decay_in_e = expanded[CHUNK:2 * CHUNK]
    decay_out_e = expanded[2 * CHUNK:]
    xdt = xs * dt_e
    xdt_b = xdt.astype(BF16)
    xw_b = (xdt * decay_out_e).astype(BF16)

    row = lax.broadcasted_iota(jnp.int32, (CHUNK, CHUNK), 0)
    col = lax.broadcasted_iota(jnp.int32, (CHUNK, CHUNK), 1)
    causal = row >= col
    lane = lax.broadcasted_iota(jnp.int32, (CHUNK, LANES), 1)
    upper_half = lane >= HEAD_DIM

    ys = []
    for g in range(SSD_GROUPS):
        cg = cm[:, g * SSD_STATE:(g + 1) * SSD_STATE]
        bg = bm[:, g * SSD_STATE:(g + 1) * SSD_STATE]
        gcols = slice(g * group_w, (g + 1) * group_w)
        cb = _dot_nt(cg, bg)
        st = st_ref[g]
        y_off = _dot(cg, st.astype(BF16)) * decay_in_e[:, gcols]
        pairs = []
        for pr in range(heads_per_group // 2):
            xp = xdt_b[:, g * group_w + pr * LANES:g * group_w + (pr + 1) * LANES]
            acc = None
            for hh in range(2):
                h = g * heads_per_group + pr * 2 + hh
                seg = acum[:, h:h + 1] - acum_t[h:h + 1, :]
                decay = jnp.where(causal, jnp.exp(jnp.where(causal, seg, 0.0)), 0.0)
                m = (cb * decay).astype(BF16)
                keep = upper_half if hh else jnp.logical_not(upper_half)
                t = _dot(m, jnp.where(keep, xp, jnp.zeros_like(xp)))
                acc = t if acc is None else acc + t
            pairs.append(acc)
        ys.append(jnp.concatenate(pairs, axis=1) + y_off)
        st_ref[g] = st * decay_in_e[CHUNK - 1:CHUNK, gcols] + _dot_tn(bg, xw_b[:, gcols])

    y = jnp.concatenate(ys, axis=1) + xs * dexp_ref[...]
    y = y * zg_ref[...]
    outs = []
    for g in range(SSD_GROUPS):
        yg = y[:, g * group_w:(g + 1) * group_w]
        gms = jnp.mean(yg * yg, axis=-1, keepdims=True)
        outs.append(yg * lax.rsqrt(gms + EPS) * nw_ref[:, g * group_w:(g + 1) * group_w])
    o_ref[...] = jnp.concatenate(outs, axis=1).astype(BF16)


def _ssd(zg, xbc, dt, conv_w, conv_b, a_log, d_exp, norm_w, ltri, eexp, shift, bsz, length):
    nc = length // CHUNK
    const = lambda b, c: (0, 0)
    rows = lambda b, c: (b * nc + c, 0)
    group_w = SSD_W // SSD_GROUPS
    need = 2 * CHUNK * (SSD_W + SSD_CONV_DIM + DT_W) * 4 * 2 + 64 * CHUNK * SSD_CONV_DIM * 4
    return pl.pallas_call(
        _ssd_body,
        grid=(bsz, nc),
        in_specs=[
            pl.BlockSpec((CHUNK, SSD_W), rows),
            pl.BlockSpec((CHUNK, SSD_CONV_DIM), rows),
            pl.BlockSpec((CHUNK, DT_W), rows),
            pl.BlockSpec((SSD_CONV, SSD_CONV_DIM), const),
            pl.BlockSpec((1, SSD_CONV_DIM), const),
            pl.BlockSpec((1, DT_W), const),
            pl.BlockSpec((1, SSD_W), const),
            pl.BlockSpec((1, SSD_W), const),
            pl.BlockSpec((CHUNK, CHUNK), const),
            pl.BlockSpec((DT_W, SSD_W), const),
            pl.BlockSpec(((SSD_CONV - 1) * CHUNK, 2 * CHUNK), const),
        ],
        out_specs=pl.BlockSpec((CHUNK, SSD_W), rows),
        out_shape=jax.ShapeDtypeStruct((bsz * length, SSD_W), BF16),
        scratch_shapes=[pltpu.VMEM((8, SSD_CONV_DIM), F32),
                        pltpu.VMEM((SSD_GROUPS, SSD_STATE, group_w), F32)],
        compiler_params=pltpu.CompilerParams(
            dimension_semantics=("parallel", "arbitrary"), vmem_limit_bytes=_vmem_limit(need)),
        name="ssd",
    )(zg, xbc, dt, conv_w, conv_b, a_log, d_exp, norm_w, ltri, eexp, shift)


def _hg_body(q_ref, lf_ref, k_ref, v_ref, g_ref, onw_ref, ltri2_ref, bd_ref,
             o_ref, st_ref, qs_ref, ks_ref, vs_ref, gs_ref, oacc_ref):
    c = pl.program_id(1)
    nsub = CHUNK // SUB

    @pl.when(c == 0)
    def _():
        st_ref[...] = jnp.zeros_like(st_ref)

    rowi = lax.broadcasted_iota(jnp.int32, (CHUNK, 1), 0)
    valid = (c * CHUNK + rowi) >= PAD
    log_f = jnp.where(valid, lf_ref[...], 0.0)
    k = jnp.where(valid, k_ref[...], 0.0)
    v = jnp.where(valid, v_ref[...], 0.0)
    q = q_ref[...]
    cums = _dot_f32_left3(ltri2_ref[...], log_f)
    gc = cums[:CHUNK]
    span = -cums[CHUNK:]
    g_last = gc[CHUNK - 1:CHUNK, :]

    st = st_ref[...]
    oacc_ref[...] = _dot_nt((q * jnp.exp(gc)).astype(BF16), st.astype(BF16))
    k_end = (k * jnp.exp(g_last - gc)).astype(BF16)
    v_b = v.astype(BF16)
    srow = lax.broadcasted_iota(jnp.int32, (HG_W, HG_W), 0) // HG_DK
    scol = lax.broadcasted_iota(jnp.int32, (HG_W, HG_W), 1) // HG_DK
    st_ref[...] = st * jnp.exp(g_last) + jnp.where(srow == scol, _dot_tn(v_b, k_end), 0.0)

    erow = lax.broadcasted_iota(jnp.int32, (HG_HEADS * SUB, HG_W), 0) // SUB
    ecol = lax.broadcasted_iota(jnp.int32, (HG_HEADS * SUB, HG_W), 1) // HG_DK
    same_head = erow == ecol
    for jb in range(nsub - 1):
        r0, r1 = jb * SUB, (jb + 1) * SUB
        g_end = gc[r1 - 1:r1, :]
        ke = k[r0:r1] * jnp.exp(g_end - gc[r0:r1])
        ke4 = jnp.where(same_head, jnp.concatenate([ke] * HG_HEADS, axis=0), 0.0).astype(BF16)
        v4 = jnp.where(same_head, jnp.concatenate([v[r0:r1]] * HG_HEADS, axis=0), 0.0).astype(BF16)
        qp = (q[r1:] * jnp.exp(gc[r1:] - g_end)).astype(BF16)
        scores = _dot_nt(qp, ke4)
        oacc_ref[r1:, :] += _dot(scores.astype(BF16), v4)

    safe = jnp.max(span) < HG_SAFE_SPAN

    @pl.when(safe)
    def _():
        qd = (q * jnp.exp(-span)).astype(BF16)
        kd = (k * jnp.exp(span)).astype(BF16)
        trow = lax.broadcasted_iota(jnp.int32, (CHUNK, CHUNK), 0)
        tcol = lax.broadcasted_iota(jnp.int32, (CHUNK, CHUNK), 1)
        pair = (trow // SUB == tcol // SUB) & (tcol <= trow)
        lane_head = lax.broadcasted_iota(jnp.int32, (CHUNK, HG_W), 1) // HG_DK
        acc = jnp.zeros((CHUNK, HG_W), F32)
        for h in range(HG_HEADS):
            of_head = lane_head == h
            sc = _dot_nt(jnp.where(of_head, qd, jnp.zeros_like(qd)), kd)
            acc = acc + _dot(jnp.where(pair, sc, 0.0).astype(BF16), jnp.where(of_head, v_b, jnp.zeros_like(v_b)))
        oacc_ref[...] += acc

    @pl.when(jnp.logical_not(safe))
    def _():
        qs_ref[...] = q
        ks_ref[...] = k
        vs_ref[...] = v
        gs_ref[...] = gc
        rin = lax.broadcasted_iota(jnp.int32, (SUB, 1), 0)

        def diag_block(ib, carry):
            r0 = pl.multiple_of(ib * SUB, SUB)
            qi = qs_ref[pl.ds(r0, SUB), :]
            gi = gs_ref[pl.ds(r0, SUB), :]
            prods = []
            for j in range(SUB):
                kj = ks_ref[pl.ds(r0 + j, 1), :]
                gj = gs_ref[pl.ds(r0 + j, 1), :]
                m = rin >= j
                prods.append(jnp.where(m, qi * kj * jnp.exp(jnp.where(m, gi - gj, 0.0)), 0.0))
            head_sums = _dot_f32_right(jnp.concatenate(prods, axis=0), bd_ref[...])
            oi = jnp.zeros((SUB, HG_W), F32)
            for j in range(SUB):
                oi = oi + head_sums[j * SUB:(j + 1) * SUB] * vs_ref[pl.ds(r0 + j, 1), :]
            oacc_ref[pl.ds(r0, SUB), :] += oi
            return carry

        lax.fori_loop(0, nsub, diag_block, 0)

    o = oacc_ref[...]
    hms = _dot_f32_right(o * o, bd_ref[...]) * (1.0 / HG_DK)
    o_ref[...] = (o * lax.rsqrt(hms + EPS) * onw_ref[...] * g_ref[...]).astype(BF16)


def _hgrn2(hq, hlf, hk, hv, hg, out_norm, ltri2, bd256, bsz, length):
    nc = length // CHUNK
    const = lambda b, c: (0, 0)
    rows = lambda b, c: (b * nc + c, 0)
    need = 64 * CHUNK * HG_W * 4
    return pl.pallas_call(
        _hg_body,
        grid=(bsz, nc),
        in_specs=[pl.BlockSpec((CHUNK, HG_W), rows)] * 5 + [
            pl.BlockSpec((1, HG_W), const),
            pl.BlockSpec((2 * CHUNK, CHUNK), const),
            pl.BlockSpec((HG_W, HG_W), const),
        ],
        out_specs=pl.BlockSpec((CHUNK, HG_W), rows),
        out_shape=jax.ShapeDtypeStruct((bsz * length, HG_W), BF16),
        scratch_shapes=[pltpu.VMEM((HG_W, HG_W), F32)] + [pltpu.VMEM((CHUNK, HG_W), F32)] * 5,
        compiler_params=pltpu.CompilerParams(
            dimension_semantics=("parallel", "arbitrary"), vmem_limit_bytes=_vmem_limit(need)),
        name="hgrn2",
    )(hq, hlf, hk, hv, hg, out_norm, ltri2, bd256)


def _out_mlp_body(osb_ref, ossd_ref, ohg_ref, h_ref, wo_ref, nw_ref, wup_ref, wdn_ref, out_ref):
    h1 = (h_ref[...]
          + _dot(osb_ref[...], wo_ref[0:SB_W, :])
          + _dot(ossd_ref[...], wo_ref[SB_W:SB_W + SSD_W, :])
          + _dot(ohg_ref[...], wo_ref[SB_W + SSD_W:, :]))
    ms = jnp.mean(h1 * h1, axis=-1, keepdims=True)
    hn = (h1 * lax.rsqrt(ms + EPS) * nw_ref[...]).astype(BF16)
    mlp = jnp.zeros_like(h1)
    for c in range(wup_ref.shape[1] // FF_BLOCK):
        u = _dot(hn, wup_ref[:, c * FF_BLOCK:(c + 1) * FF_BLOCK])
        act = jnp.square(jnp.maximum(u, 0.0)).astype(BF16)
        mlp = mlp + _dot(act, wdn_ref[c * FF_BLOCK:(c + 1) * FF_BLOCK, :])
    out_ref[...] = h1 + mlp


def _out_mlp(o_sb, o_ssd, o_hg, h, w_out, norm_w, w_up, w_down):
    rows, d = h.shape
    d_ff = w_up.shape[1]
    tm = _row_tile(rows, 512)
    const = lambda i: (0, 0)
    tile = lambda i: (i, 0)
    weights = (w_out.shape[0] * d + 2 * d * d_ff) * 2
    need = 2 * weights + 2 * tm * (2 * d * 4 + (SB_W + SSD_W + HG_W) * 2) + 6 * tm * FF_BLOCK * 4
    return pl.pallas_call(
        _out_mlp_body,
        grid=(rows // tm,),
        in_specs=[
            pl.BlockSpec((tm, SB_W), tile),
            pl.BlockSpec((tm, SSD_W), tile),
            pl.BlockSpec((tm, HG_W), tile),
            pl.BlockSpec((tm, d), tile),
            pl.BlockSpec(w_out.shape, const),
            pl.BlockSpec((1, d), const),
            pl.BlockSpec(w_up.shape, const),
            pl.BlockSpec(w_down.shape, const),
        ],
        out_specs=pl.BlockSpec((tm, d), tile),
        out_shape=jax.ShapeDtypeStruct((rows, d), F32),
        compiler_params=pltpu.CompilerParams(
            dimension_semantics=("parallel",), vmem_limit_bytes=_vmem_limit(need)),
        name="out_mlp",
    )(o_sb, o_ssd, o_hg, h, w_out, norm_w, w_up, w_down)


def _block_diag_ones(n, block):
    idx = np.arange(n) // block
    return jnp.asarray(idx[:, None] == idx[None, :], BF16)


def _constants():
    t = np.arange(CHUNK)
    ltri = jnp.asarray(t[None, :] <= t[:, None], BF16)
    later = (t[:, None] > t[None, :])
    usum = jnp.asarray(np.concatenate([later, np.ones((CHUNK, CHUNK), bool)], axis=1), BF16)
    eexp = np.zeros((DT_W, SSD_W), bool)
    for h in range(SSD_HEADS):
        eexp[h, h * HEAD_DIM:(h + 1) * HEAD_DIM] = True
    in_sub = (t[None, :] <= t[:, None]) & (t[None, :] // SUB == t[:, None] // SUB)
    ltri2 = jnp.asarray(np.concatenate([t[None, :] <= t[:, None], in_sub], axis=0), BF16)
    shift = np.concatenate([t[None, :] == t[:, None] - k for k in range(1, SSD_CONV)], axis=0)
    shift = jnp.asarray(np.concatenate([shift, shift], axis=1), BF16)
    return ltri, ltri2, usum, jnp.asarray(eexp, BF16), shift


def _pack_w_in(w):
    d = w.shape[0]
    dt_lo = 3 * SB_W + 2 * SSD_W + 2 * SSD_BC_W
    dt_hi = dt_lo + SSD_HEADS
    return jnp.concatenate(
        [w[:, :dt_lo], w[:, dt_lo:dt_hi], jnp.zeros((d, DT_W - SSD_HEADS), w.dtype), w[:, dt_hi:]],
        axis=1).astype(BF16)


def _pad_lanes(v, width):
    return jnp.pad(v.astype(F32), (0, width - v.shape[0]))[None, :]


def kernel(x, meta_tokens, hg_lb_logits, norm_mix_w, w_in, sb_q_norm, sb_k_norm, sb_out_norm, ssd_conv_w,
           ssd_conv_b, ssd_dt_bias, ssd_A_log, ssd_D, ssd_norm_w, hg_out_norm, w_out, norm_mlp_w, w_up, w_down):
    bsz, seq, d = x.shape
    depth = w_in.shape[0]
    length = seq + CHUNK
    lead = jnp.concatenate([jnp.zeros((PAD, d), x.dtype), meta_tokens.astype(x.dtype)], axis=0)
    h = jnp.concatenate([jnp.broadcast_to(lead[None], (bsz, CHUNK, d)), x], axis=1).reshape(bsz * length, d)

    probs = jax.nn.softmax(hg_lb_logits.astype(F32), axis=0)
    lbs = jnp.concatenate([jnp.zeros_like(probs[0:1]), jnp.cumsum(probs, axis=0)[:-1]], axis=0)

    ltri, ltri2, usum, eexp, shift = _constants()
    bd256 = _block_diag_ones(SB_W, HEAD_DIM)
    bd128 = _block_diag_ones(LANES, HEAD_DIM)

    for l in range(depth):
        q, k, v, zg, xbc, dt, hq, hlf, hk, hv, hg = _in_proj(
            h, norm_mix_w[l][None, :], _pack_w_in(w_in[l]),
            jnp.tile(sb_q_norm[l], SB_HEADS)[None, :], jnp.tile(sb_k_norm[l], SB_HEADS)[None, :], bd256,
            _pad_lanes(ssd_dt_bias[l], DT_W), lbs[l][None, :])
        o_sb = _sb_attn(q, k, v, sb_out_norm[l].reshape(1, SB_W), usum, bd128, bsz, length)
        o_ssd = _ssd(zg, xbc, dt, ssd_conv_w[l], ssd_conv_b[l][None, :],
                     _pad_lanes(ssd_A_log[l], DT_W), jnp.repeat(ssd_D[l].astype(F32), HEAD_DIM)[None, :],
                     ssd_norm_w[l].reshape(1, SSD_W), ltri, eexp, shift, bsz, length)
        o_hg = _hgrn2(hq, hlf, hk, hv, hg, hg_out_norm[l].reshape(1, HG_W), ltri2, bd256, bsz, length)
        h = _out_mlp(o_sb, o_ssd, o_hg, h, w_out[l].astype(BF16), norm_mlp_w[l][None, :],
                     w_up[l].astype(BF16), w_down[l].astype(BF16))
    return h.reshape(bsz, length, d)[:, CHUNK:]
```

```python
import functools

import numpy as np
import jax
import jax.numpy as jnp
from jax import lax
from jax.experimental import pallas as pl
from jax.experimental.pallas import tpu as pltpu

F32 = jnp.float32
BF16 = jnp.bfloat16

N_META = 16
CHUNK = 128
PAD = CHUNK - N_META
HEAD_DIM = 64
SB_HEADS = 4
SB_W = SB_HEADS * HEAD_DIM
SSD_HEADS = 8
SSD_W = SSD_HEADS * HEAD_DIM
SSD_GROUPS = 2
SSD_STATE = 128
SSD_CONV = 4
SSD_BC_W = SSD_GROUPS * SSD_STATE
SSD_CONV_DIM = SSD_W + 2 * SSD_BC_W
HG_HEADS = 4
HG_DK = 64
HG_W = HG_HEADS * HG_DK
EPS = 1e-6
TINY = 1e-30
LOG2E = 1.4426950408889634
HG_SAFE_SPAN = 60.0
SUB = 16
SB_BLOCK = 3 * CHUNK
SB_DEAD_MASS = 150.0
LANES = 128
DT_W = LANES
FF_BLOCK = 1024
VMEM_CAP_V7X = 64 * 1024 * 1024

_C_Q, _C_K, _C_V = 0, SB_W, 2 * SB_W
_C_Z = 3 * SB_W
_C_XBC = _C_Z + SSD_W
_C_DT = _C_XBC + SSD_CONV_DIM
_C_HQ = _C_DT + DT_W
_C_HF = _C_HQ + HG_W
_C_HI = _C_HF + HG_W
_C_HG = _C_HI + HG_W
D_IN_PACKED = _C_HG + HG_W


def _vmem_limit(need_bytes):
    return int(min(max(need_bytes, 32 * 1024 * 1024), VMEM_CAP_V7X - 6 * 1024 * 1024))


def _seqs_per_step(bsz):
    return 2 if bsz % 2 == 0 else 1


def _row_tile(rows, target):
    t = min(target, rows)
    while rows % t:
        t -= CHUNK
    return t


def _dot(a, b):
    return jnp.dot(a, b, preferred_element_type=F32)


def _dot_nt(a, b):
    return lax.dot_general(a, b, (((1,), (1,)), ((), ())), preferred_element_type=F32)


def _dot_tn(a, b):
    return lax.dot_general(a, b, (((0,), (0,)), ((), ())), preferred_element_type=F32)


def _split2(x):
    hi = x.astype(BF16)
    lo = (x - hi.astype(F32)).astype(BF16)
    return hi, lo


def _split3(x):
    hi = x.astype(BF16)
    r = x - hi.astype(F32)
    mid = r.astype(BF16)
    lo = (r - mid.astype(F32)).astype(BF16)
    return hi, mid, lo


def _dot_f32_right(x, m):
    hi, lo = _split2(x)
    return _dot(hi, m) + _dot(lo, m)


def _dot_f32_left3(m, x):
    hi, mid, lo = _split3(x)
    return _dot(m, hi) + _dot(m, mid) + _dot(m, lo)


def _softplus(x):
    return jnp.maximum(x, 0.0) + jnp.log(1.0 + jnp.exp(-jnp.abs(x)))


def _sigmoid(x):
    return 1.0 / (1.0 + jnp.exp(-x))


def _silu(x):
    return x * _sigmoid(x)


def _in_proj_body(tiles_per_seq, h_ref, nw_ref, w_ref, qn_ref, kn_ref, bd_ref, dtb_ref, lb_ref,
                  q_ref, k_ref, v_ref, zg_ref, xbc_ref, dt_ref, hq_ref, hlf_ref, hk_ref, hv_ref, hg_ref):
    tm = h_ref.shape[0]
    rowi = lax.broadcasted_iota(jnp.int32, (tm, 1), 0)
    valid = (pl.program_id(0) % tiles_per_seq) * tm + rowi >= PAD
    x = h_ref[...]
    ms = jnp.mean(x * x, axis=-1, keepdims=True)
    hn = (x * lax.rsqrt(ms + EPS) * nw_ref[...]).astype(BF16)

    def seg(lo, width):
        return _dot(hn, w_ref[:, lo:lo + width])

    def head_norm(t, w):
        hms = _dot_f32_right(t * t, bd_ref[...]) * (1.0 / HEAD_DIM)
        return t * lax.rsqrt(hms + EPS) * w

    q = head_norm(seg(_C_Q, SB_W), qn_ref[...])
    q_ref[...] = (q * (HEAD_DIM ** -0.5 * LOG2E)).astype(BF16)
    k_ref[...] = jnp.where(valid, head_norm(seg(_C_K, SB_W), kn_ref[...]), 0.0).astype(BF16)
    v_ref[...] = jnp.where(valid, seg(_C_V, SB_W), 0.0).astype(BF16)
    zg_ref[...] = _silu(seg(_C_Z, SSD_W))
    xbc_ref[...] = seg(_C_XBC, SSD_CONV_DIM)
    dt_ref[...] = _softplus(seg(_C_DT, DT_W) + dtb_ref[...])
    hq_ref[...] = _silu(seg(_C_HQ, HG_W))
    lb = lb_ref[...]
    fl = seg(_C_HF, HG_W)
    hlf_ref[...] = jnp.log(jnp.maximum(lb + (1.0 - lb) * _sigmoid(fl), TINY))
    hk_ref[...] = (1.0 - lb) * _sigmoid(-fl)
    hv_ref[...] = seg(_C_HI, HG_W)
    hg_ref[...] = _silu(seg(_C_HG, HG_W))


def _in_proj(h, norm_w, w_packed, qn, kn, bd256, dt_bias, lb, length):
    rows, d = h.shape
    tm = _row_tile(length, 512)
    widths = (SB_W, SB_W, SB_W, SSD_W, SSD_CONV_DIM, DT_W, HG_W, HG_W, HG_W, HG_W, HG_W)
    dtypes = (BF16, BF16, BF16, F32, F32, F32, F32, F32, F32, F32, F32)
    const = lambda i: (0, 0)
    out_bytes = sum(w * jnp.dtype(t).itemsize for w, t in zip(widths, dtypes)) * tm
    need = 2 * (tm * d * 4 + d * D_IN_PACKED * 2 + out_bytes) + tm * SSD_CONV_DIM * 4 * 4
    return pl.pallas_call(
        functools.partial(_in_proj_body, length // tm),
        grid=(rows // tm,),
        in_specs=[
            pl.BlockSpec((tm, d), lambda i: (i, 0)),
            pl.BlockSpec((1, d), const),
            pl.BlockSpec((d, D_IN_PACKED), const),
            pl.BlockSpec((1, SB_W), const),
            pl.BlockSpec((1, SB_W), const),
            pl.BlockSpec((SB_W, SB_W), const),
            pl.BlockSpec((1, DT_W), const),
            pl.BlockSpec((1, HG_W), const),
        ],
        out_specs=[pl.BlockSpec((tm, w), lambda i: (i, 0)) for w in widths],
        out_shape=[jax.ShapeDtypeStruct((rows, w), t) for w, t in zip(widths, dtypes)],
        compiler_params=pltpu.CompilerParams(
            dimension_semantics=("parallel",), vmem_limit_bytes=_vmem_limit(need)),
        name="in_proj",
    )(h, norm_w, w_packed, qn, kn, bd256, dt_bias, lb)


def _sb_body(q_ref, k_ref, v_ref, onw_ref, usum_ref, bd_ref, o_ref, acc_ref, carry_ref):
    ib = pl.program_id(2)
    nsub = SB_BLOCK // CHUNK
    lane = lax.broadcasted_iota(jnp.int32, (SB_BLOCK, LANES), 1)
    q = q_ref[...]
    zero = jnp.zeros_like(q)
    q2 = jnp.concatenate([jnp.where(lane < HEAD_DIM, q, zero), jnp.where(lane >= HEAD_DIM, q, zero)], axis=0)
    acc_ref[...] = jnp.zeros_like(acc_ref)
    carry_ref[...] = jnp.zeros_like(carry_ref)

    def block_rows(jb):
        return pl.ds(pl.multiple_of(jb * SB_BLOCK, SB_BLOCK), SB_BLOCK)

    def logits(jb):
        return _dot_nt(q2, k_ref[block_rows(jb), :])

    def accumulate(w, jb):
        acc_ref[...] += _dot(w, v_ref[block_rows(jb), :])

    def weights(z, mask):
        sp = jnp.maximum(z, 0.0) + jnp.log2(1.0 + jnp.exp2(-jnp.abs(z)))
        drop = sp if mask is None else jnp.where(mask, sp, 0.0)
        drop = drop.astype(BF16)
        carry = carry_ref[...]
        ws = [None] * nsub
        for m in reversed(range(nsub)):
            cols = slice(m * CHUNK, (m + 1) * CHUNK)
            sums = _dot(drop[:, cols], usum_ref[...])
            log_w = (z[:, cols] - sp[:, cols]) - sums[:, :CHUNK] - carry
            carry = carry + sums[:, CHUNK:]
            if mask is None:
                w = jnp.exp2(log_w)
            else:
                w = jnp.where(mask[:, cols], jnp.exp2(jnp.where(mask[:, cols], log_w, 0.0)), 0.0)
            ws[m] = w.astype(BF16)
        carry_ref[...] = carry
        return jnp.concatenate(ws, axis=1)

    row = lax.broadcasted_iota(jnp.int32, (SB_BLOCK, SB_BLOCK), 0)
    col = lax.broadcasted_iota(jnp.int32, (SB_BLOCK, SB_BLOCK), 1)
    causal = col < row
    accumulate(weights(logits(ib), jnp.concatenate([causal, causal], axis=0)), ib)

    def live():
        return (jnp.min(carry_ref[...]) < SB_DEAD_MASS).astype(jnp.int32)

    def cond(state):
        jb, alive = state
        return jnp.logical_and(jb >= 0, alive > 0)

    def body(state):
        jb, _ = state
        accumulate(weights(logits(jb), None), jb)
        return jb - 1, live()

    lax.while_loop(cond, body, (ib - 1, live()))

    o = jnp.where(lane < HEAD_DIM, acc_ref[0:SB_BLOCK, :], acc_ref[SB_BLOCK:, :])
    hms = _dot_f32_right(o * o, bd_ref[...]) * (1.0 / HEAD_DIM)
    o_ref[...] = (o * lax.rsqrt(hms + EPS) * onw_ref[...]).astype(BF16)


def _sb_attn(q, k, v, out_norm, usum, bd128, bsz, length):
    assert length % SB_BLOCK == 0
    nb = length // SB_BLOCK
    const = lambda b, p, i: (0, 0)
    need = 2 * (2 * length * LANES * 2) + 24 * 2 * SB_BLOCK * SB_BLOCK * 4
    return pl.pallas_call(
        _sb_body,
        grid=(bsz, SB_W // LANES, nb),
        in_specs=[
            pl.BlockSpec((SB_BLOCK, LANES), lambda b, p, i: (b * nb + i, p)),
            pl.BlockSpec((length, LANES), lambda b, p, i: (b, p)),
            pl.BlockSpec((length, LANES), lambda b, p, i: (b, p)),
            pl.BlockSpec((1, LANES), lambda b, p, i: (0, p)),
            pl.BlockSpec((CHUNK, 2 * CHUNK), const),
            pl.BlockSpec((LANES, LANES), const),
        ],
        out_specs=pl.BlockSpec((SB_BLOCK, LANES), lambda b, p, i: (b * nb + i, p)),
        out_shape=jax.ShapeDtypeStruct((bsz * length, SB_W), BF16),
        scratch_shapes=[pltpu.VMEM((2 * SB_BLOCK, LANES), F32), pltpu.VMEM((2 * SB_BLOCK, CHUNK), F32)],
        compiler_params=pltpu.CompilerParams(
            dimension_semantics=("parallel", "parallel", "arbitrary"), vmem_limit_bytes=_vmem_limit(need)),
        name="sb_attn",
    )(q, k, v, out_norm, usum, bd128)


def _ssd_body(zg_ref, xbc_ref, dt_ref, cw_ref, cb_ref, alog_ref, dexp_ref, nw_ref, ltri_ref, eexp_ref, shift_ref,
              o_ref, tail_ref, st_ref):
    c = pl.program_id(1)

    @pl.when(c == 0)
    def _():
        st_ref[...] = jnp.zeros_like(st_ref)
        tail_ref[...] = jnp.zeros_like(tail_ref)

    for bb in range(zg_ref.shape[0]):
        _ssd_chunk(bb, c, zg_ref, xbc_ref, dt_ref, cw_ref, cb_ref, alog_ref, dexp_ref, nw_ref, ltri_ref, eexp_ref,
                   shift_ref, o_ref, tail_ref, st_ref)


def _ssd_chunk(bb, c, zg_ref, xbc_ref, dt_ref, cw_ref, cb_ref, alog_ref, dexp_ref, nw_ref, ltri_ref, eexp_ref,
               shift_ref, o_ref, tail_ref, st_ref):
    heads_per_group = SSD_HEADS // SSD_GROUPS
    group_w = SSD_W // SSD_GROUPS
    taps = SSD_CONV - 1

    rowi = lax.broadcasted_iota(jnp.int32, (CHUNK, 1), 0)
    valid = (c * CHUNK + rowi) >= PAD
    u = jnp.where(valid, xbc_ref[bb], 0.0)
    u_hi, u_lo = _split2(u)
    shifted = _dot(shift_ref[...], jnp.concatenate([u_hi, u_lo], axis=0))
    conv = cb_ref[...] + cw_ref[taps:taps + 1, :] * u
    for kk in range(1, taps + 1):
        conv = conv + cw_ref[taps - kk:taps - kk + 1, :] * shifted[(kk - 1) * CHUNK:kk * CHUNK]
    tail = tail_ref[bb]
    row8 = lax.broadcasted_iota(jnp.int32, (8, 1), 0)
    head_fix = jnp.zeros((8, SSD_CONV_DIM), F32)
    for kk in range(1, taps + 1):
        head_fix = head_fix + cw_ref[taps - kk:taps - kk + 1, :] * jnp.where(row8 < kk, pltpu.roll(tail, kk, 0), 0.0)
    conv = jnp.concatenate([conv[:8] + head_fix, conv[8:]], axis=0)
    tail_ref[bb] = u[CHUNK - 8:]
    act = _silu(conv)
    xs = act[:, :SSD_W]
    bm = act[:, SSD_W:SSD_W + SSD_BC_W].astype(BF16)
    cm = act[:, SSD_W + SSD_BC_W:].astype(BF16)

    dt = jnp.where(valid, dt_ref[bb], 0.0)
    a = dt * (-jnp.exp(alog_ref[...]))
    acum = _dot_f32_left3(ltri_ref[...], a)
    acum_t = acum.T
    a_last = acum[CHUNK - 1:CHUNK, :]
    per_head = jnp.concatenate([dt, jnp.exp(acum), jnp.exp(a_last - acum)], axis=0)
    expanded = _dot_f32_right(per_head, eexp_ref[...])
    dt_e = expanded[:CHUNK]
    decay_in_e = expanded[CHUNK:2 * CHUNK]
    decay_out_e = expanded[2 * CHUNK:]
    xdt = xs * dt_e
    xdt_b = xdt.astype(BF16)
    xw_b = (xdt * decay_out_e).astype(BF16)

    row = lax.broadcasted_iota(jnp.int32, (CHUNK, CHUNK), 0)
    col = lax.broadcasted_iota(jnp.int32, (CHUNK, CHUNK), 1)
    causal = row >= col
    lane = lax.broadcasted_iota(jnp.int32, (CHUNK, LANES), 1)
    upper_half = lane >= HEAD_DIM

    ys = []
    for g in range(SSD_GROUPS):
        cg = cm[:, g * SSD_STATE:(g + 1) * SSD_STATE]
        bg = bm[:, g * SSD_STATE:(g + 1) * SSD_STATE]
        gcols = slice(g * group_w, (g + 1) * group_w)
        cb = _dot_nt(cg, bg)
        st = st_ref[bb, g]
        y_off = _dot(cg, st.astype(BF16)) * decay_in_e[:, gcols]
        pairs = []
        for pr in range(heads_per_group // 2):
            xp = xdt_b[:, g * group_w + pr * LANES:g * group_w + (pr + 1) * LANES]
            acc = None
            for hh in range(2):
                h = g * heads_per_group + pr * 2 + hh
                seg = acum[:, h:h + 1] - acum_t[h:h + 1, :]
                decay = jnp.where(causal, jnp.exp(jnp.where(causal, seg, 0.0)), 0.0)
                m = (cb * decay).astype(BF16)
                keep = upper_half if hh else jnp.logical_not(upper_half)
                t = _dot(m, jnp.where(keep, xp, jnp.zeros_like(xp)))
                acc = t if acc is None else acc + t
            pairs.append(acc)
        ys.append(jnp.concatenate(pairs, axis=1) + y_off)
        st_ref[bb, g] = st * decay_in_e[CHUNK - 1:CHUNK, gcols] + _dot_tn(bg, xw_b[:, gcols])

    y = jnp.concatenate(ys, axis=1) + xs * dexp_ref[...]
    y = y * zg_ref[bb]
    outs = []
    for g in range(SSD_GROUPS):
        yg = y[:, g * group_w:(g + 1) * group_w]
        gms = jnp.mean(yg * yg, axis=-1, keepdims=True)
        outs.append(yg * lax.rsqrt(gms + EPS) * nw_ref[:, g * group_w:(g + 1) * group_w])
    o_ref[bb] = jnp.concatenate(outs, axis=1).astype(BF16)


def _ssd(zg, xbc, dt, conv_w, conv_b, a_log, d_exp, norm_w, ltri, eexp, shift, bsz, length):
    nc = length // CHUNK
    per = _seqs_per_step(bsz)
    const = lambda b, c: (0, 0)
    rows = lambda b, c: (b, c, 0)
    group_w = SSD_W // SSD_GROUPS
    need = per * (2 * CHUNK * (SSD_W + SSD_CONV_DIM + DT_W) * 4 * 2 + 64 * CHUNK * SSD_CONV_DIM * 4)
    out = pl.pallas_call(
        _ssd_body,
        grid=(bsz // per, nc),
        in_specs=[
            pl.BlockSpec((per, CHUNK, SSD_W), rows),
            pl.BlockSpec((per, CHUNK, SSD_CONV_DIM), rows),
            pl.BlockSpec((per, CHUNK, DT_W), rows),
            pl.BlockSpec((SSD_CONV, SSD_CONV_DIM), const),
            pl.BlockSpec((1, SSD_CONV_DIM), const),
            pl.BlockSpec((1, DT_W), const),
            pl.BlockSpec((1, SSD_W), const),
            pl.BlockSpec((1, SSD_W), const),
            pl.BlockSpec((CHUNK, CHUNK), const),
            pl.BlockSpec((DT_W, SSD_W), const),
            pl.BlockSpec(((SSD_CONV - 1) * CHUNK, 2 * CHUNK), const),
        ],
        out_specs=pl.BlockSpec((per, CHUNK, SSD_W), rows),
        out_shape=jax.ShapeDtypeStruct((bsz, length, SSD_W), BF16),
        scratch_shapes=[pltpu.VMEM((per, 8, SSD_CONV_DIM), F32),
                        pltpu.VMEM((per, SSD_GROUPS, SSD_STATE, group_w), F32)],
        compiler_params=pltpu.CompilerParams(
            dimension_semantics=("parallel", "arbitrary"), vmem_limit_bytes=_vmem_limit(need)),
        name="ssd",
    )(zg.reshape(bsz, length, SSD_W), xbc.reshape(bsz, length, SSD_CONV_DIM), dt.reshape(bsz, length, DT_W),
      conv_w, conv_b, a_log, d_exp, norm_w, ltri, eexp, shift)
    return out.reshape(bsz * length, SSD_W)


def _hg_body(q_ref, lf_ref, k_ref, v_ref, g_ref, onw_ref, ltri2_ref, bd_ref,
             o_ref, st_ref, qs_ref, ks_ref, vs_ref, gs_ref, oacc_ref):
    c = pl.program_id(1)
    nsub = CHUNK // SUB
    per = q_ref.shape[0]

    @pl.when(c == 0)
    def _():
        st_ref[...] = jnp.zeros_like(st_ref)

    rowi = lax.broadcasted_iota(jnp.int32, (CHUNK, 1), 0)
    valid = (c * CHUNK + rowi) >= PAD
    srow = lax.broadcasted_iota(jnp.int32, (HG_W, HG_W), 0) // HG_DK
    scol = lax.broadcasted_iota(jnp.int32, (HG_W, HG_W), 1) // HG_DK
    erow = lax.broadcasted_iota(jnp.int32, (HG_HEADS * SUB, HG_W), 0) // SUB
    ecol = lax.broadcasted_iota(jnp.int32, (HG_HEADS * SUB, HG_W), 1) // HG_DK
    same_head = erow == ecol

    seqs = []
    for bb in range(per):
        log_f = jnp.where(valid, lf_ref[bb], 0.0)
        k = jnp.where(valid, k_ref[bb], 0.0)
        v = jnp.where(valid, v_ref[bb], 0.0)
        q = q_ref[bb]
        cums = _dot_f32_left3(ltri2_ref[...], log_f)
        gc = cums[:CHUNK]
        span = -cums[CHUNK:]
        g_last = gc[CHUNK - 1:CHUNK, :]

        st = st_ref[bb]
        oacc_ref[bb] = _dot_nt((q * jnp.exp(gc)).astype(BF16), st.astype(BF16))
        k_end = (k * jnp.exp(g_last - gc)).astype(BF16)
        v_b = v.astype(BF16)
        st_ref[bb] = st * jnp.exp(g_last) + jnp.where(srow == scol, _dot_tn(v_b, k_end), 0.0)

        for jb in range(nsub - 1):
            r0, r1 = jb * SUB, (jb + 1) * SUB
            g_end = gc[r1 - 1:r1, :]
            ke = k[r0:r1] * jnp.exp(g_end - gc[r0:r1])
            ke4 = jnp.where(same_head, jnp.concatenate([ke] * HG_HEADS, axis=0), 0.0).astype(BF16)
            v4 = jnp.where(same_head, jnp.concatenate([v[r0:r1]] * HG_HEADS, axis=0), 0.0).astype(BF16)
            qp = (q[r1:] * jnp.exp(gc[r1:] - g_end)).astype(BF16)
            scores = _dot_nt(qp, ke4)
            oacc_ref[bb, r1:, :] += _dot(scores.astype(BF16), v4)
        seqs.append((q, k, v, v_b, gc, span))

    widest = seqs[0][5]
    for seq in seqs[1:]:
        widest = jnp.maximum(widest, seq[5])
    safe = jnp.max(widest) < HG_SAFE_SPAN

    @pl.when(safe)
    def _():
        trow = lax.broadcasted_iota(jnp.int32, (CHUNK, CHUNK), 0)
        tcol = lax.broadcasted_iota(jnp.int32, (CHUNK, CHUNK), 1)
        pair = (trow // SUB == tcol // SUB) & (tcol <= trow)
        lane_head = lax.broadcasted_iota(jnp.int32, (CHUNK, HG_W), 1) // HG_DK
        for bb, (q, k, v, v_b, gc, span) in enumerate(seqs):
            qd = (q * jnp.exp(-span)).astype(BF16)
            kd = (k * jnp.exp(span)).astype(BF16)
            acc = jnp.zeros((CHUNK, HG_W), F32)
            for h in range(HG_HEADS):
                of_head = lane_head == h
                sc = _dot_nt(jnp.where(of_head, qd, jnp.zeros_like(qd)), kd)
                acc = acc + _dot(jnp.where(pair, sc, 0.0).astype(BF16), jnp.where(of_head, v_b, jnp.zeros_like(v_b)))
            oacc_ref[bb] += acc

    @pl.when(jnp.logical_not(safe))
    def _():
        rin = lax.broadcasted_iota(jnp.int32, (SUB, 1), 0)
        for bb, (q, k, v, v_b, gc, span) in enumerate(seqs):
            qs_ref[...] = q
            ks_ref[...] = k
            vs_ref[...] = v
            gs_ref[...] = gc

            def diag_block(ib, carry, bb=bb):
                r0 = pl.multiple_of(ib * SUB, SUB)
                qi = qs_ref[pl.ds(r0, SUB), :]
                gi = gs_ref[pl.ds(r0, SUB), :]
                prods = []
                for j in range(SUB):
                    kj = ks_ref[pl.ds(r0 + j, 1), :]
                    gj = gs_ref[pl.ds(r0 + j, 1), :]
                    m = rin >= j
                    prods.append(jnp.where(m, qi * kj * jnp.exp(jnp.where(m, gi - gj, 0.0)), 0.0))
                head_sums = _dot_f32_right(jnp.concatenate(prods, axis=0), bd_ref[...])
                oi = jnp.zeros((SUB, HG_W), F32)
                for j in range(SUB):
                    oi = oi + head_sums[j * SUB:(j + 1) * SUB] * vs_ref[pl.ds(r0 + j, 1), :]
                oacc_ref[bb, pl.ds(r0, SUB), :] += oi
                return carry

            lax.fori_loop(0, nsub, diag_block, 0)

    for bb in range(per):
        o = oacc_ref[bb]
        hms = _dot_f32_right(o * o, bd_ref[...]) * (1.0 / HG_DK)
        o_ref[bb] = (o * lax.rsqrt(hms + EPS) * onw_ref[...] * g_ref[bb]).astype(BF16)


def _hgrn2(hq, hlf, hk, hv, hg, out_norm, ltri2, bd256, bsz, length):
    nc = length // CHUNK
    per = _seqs_per_step(bsz)
    const = lambda b, c: (0, 0)
    rows = lambda b, c: (b, c, 0)
    need = per * 64 * CHUNK * HG_W * 4
    seq3 = lambda t: t.reshape(bsz, length, HG_W)
    out = pl.pallas_call(
        _hg_body,
        grid=(bsz // per, nc),
        in_specs=[pl.BlockSpec((per, CHUNK, HG_W), rows)] * 5 + [
            pl.BlockSpec((1, HG_W), const),
            pl.BlockSpec((2 * CHUNK, CHUNK), const),
            pl.BlockSpec((HG_W, HG_W), const),
        ],
        out_specs=pl.BlockSpec((per, CHUNK, HG_W), rows),
        out_shape=jax.ShapeDtypeStruct((bsz, length, HG_W), BF16),
        scratch_shapes=[pltpu.VMEM((per, HG_W, HG_W), F32)] + [pltpu.VMEM((CHUNK, HG_W), F32)] * 4
        + [pltpu.VMEM((per, CHUNK, HG_W), F32)],
        compiler_params=pltpu.CompilerParams(
            dimension_semantics=("parallel", "arbitrary"), vmem_limit_bytes=_vmem_limit(need)),
        name="hgrn2",
    )(seq3(hq), seq3(hlf), seq3(hk), seq3(hv), seq3(hg), out_norm, ltri2, bd256)
    return out.reshape(bsz * length, HG_W)


def _out_mlp_body(osb_ref, ossd_ref, ohg_ref, h_ref, wo_ref, nw_ref, wup_ref, wdn_ref, out_ref):
    h1 = (h_ref[...]
          + _dot(osb_ref[...], wo_ref[0:SB_W, :])
          + _dot(ossd_ref[...], wo_ref[SB_W:SB_W + SSD_W, :])
          + _dot(ohg_ref[...], wo_ref[SB_W + SSD_W:, :]))
    ms = jnp.mean(h1 * h1, axis=-1, keepdims=True)
    hn = (h1 * lax.rsqrt(ms + EPS) * nw_ref[...]).astype(BF16)
    mlp = jnp.zeros_like(h1)
    for c in range(wup_ref.shape[1] // FF_BLOCK):
        u = _dot(hn, wup_ref[:, c * FF_BLOCK:(c + 1) * FF_BLOCK])
        act = jnp.square(jnp.maximum(u, 0.0)).astype(BF16)
        mlp = mlp + _dot(act, wdn_ref[c * FF_BLOCK:(c + 1) * FF_BLOCK, :])
    out_ref[...] = h1 + mlp


def _out_mlp(o_sb, o_ssd, o_hg, h, w_out, norm_w, w_up, w_down):
    rows, d = h.shape
    d_ff = w_up.shape[1]
    tm = _row_tile(rows, 512)
    const = lambda i: (0, 0)
    tile = lambda i: (i, 0)
    weights = (w_out.shape[0] * d + 2 * d * d_ff) * 2
    need = 2 * weights + 2 * tm * (2 * d * 4 + (SB_W + SSD_W + HG_W) * 2) + 6 * tm * FF_BLOCK * 4
    return pl.pallas_call(
        _out_mlp_body,
        grid=(rows // tm,),
        in_specs=[
            pl.BlockSpec((tm, SB_W), tile),
            pl.BlockSpec((tm, SSD_W), tile),
            pl.BlockSpec((tm, HG_W), tile),
            pl.BlockSpec((tm, d), tile),
            pl.BlockSpec(w_out.shape, const),
            pl.BlockSpec((1, d), const),
            pl.BlockSpec(w_up.shape, const),
            pl.BlockSpec(w_down.shape, const),
        ],
        out_specs=pl.BlockSpec((tm, d), tile),
        out_shape=jax.ShapeDtypeStruct((rows, d), F32),
        compiler_params=pltpu.CompilerParams(
            dimension_semantics=("parallel",), vmem_limit_bytes=_vmem_limit(need)),
        name="out_mlp",
    )(o_sb, o_ssd, o_hg, h, w_out, norm_w, w_up, w_down)


def _block_diag_ones(n, block):
    idx = np.arange(n) // block
    return jnp.asarray(idx[:, None] == idx[None, :], BF16)


def _constants():
    t = np.arange(CHUNK)
    ltri = jnp.asarray(t[None, :] <= t[:, None], BF16)
    later = (t[:, None] > t[None, :])
    usum = jnp.asarray(np.concatenate([later, np.ones((CHUNK, CHUNK), bool)], axis=1), BF16)
    eexp = np.zeros((DT_W, SSD_W), bool)
    for h in range(SSD_HEADS):
        eexp[h, h * HEAD_DIM:(h + 1) * HEAD_DIM] = True
    in_sub = (t[None, :] <= t[:, None]) & (t[None, :] // SUB == t[:, None] // SUB)
    ltri2 = jnp.asarray(np.concatenate([t[None, :] <= t[:, None], in_sub], axis=0), BF16)
    shift = np.concatenate([t[None, :] == t[:, None] - k for k in range(1, SSD_CONV)], axis=0)
    shift = jnp.asarray(np.concatenate([shift, shift], axis=1), BF16)
    return ltri, ltri2, usum, jnp.asarray(eexp, BF16), shift


def _pack_w_in(w):
    d = w.shape[0]
    dt_lo = 3 * SB_W + 2 * SSD_W + 2 * SSD_BC_W
    dt_hi = dt_lo + SSD_HEADS
    return jnp.concatenate(
        [w[:, :dt_lo], w[:, dt_lo:dt_hi], jnp.zeros((d, DT_W - SSD_HEADS), w.dtype), w[:, dt_hi:]],
        axis=1).astype(BF16)


def _pad_lanes(v, width):
    return jnp.pad(v.astype(F32), (0, width - v.shape[0]))[None, :]


def kernel(x, meta_tokens, hg_lb_logits, norm_mix_w, w_in, sb_q_norm, sb_k_norm, sb_out_norm, ssd_conv_w,
           ssd_conv_b, ssd_dt_bias, ssd_A_log, ssd_D, ssd_norm_w, hg_out_norm, w_out, norm_mlp_w, w_up, w_down):
    bsz, seq, d = x.shape
    depth = w_in.shape[0]
    length = seq + CHUNK
    lead = jnp.concatenate([jnp.zeros((PAD, d), x.dtype), meta_tokens.astype(x.dtype)], axis=0)
    h = jnp.concatenate([jnp.broadcast_to(lead[None], (bsz, CHUNK, d)), x], axis=1).reshape(bsz * length, d)

    probs = jax.nn.softmax(hg_lb_logits.astype(F32), axis=0)
    lbs = jnp.concatenate([jnp.zeros_like(probs[0:1]), jnp.cumsum(probs, axis=0)[:-1]], axis=0)

    ltri, ltri2, usum, eexp, shift = _constants()
    bd256 = _block_diag_ones(SB_W, HEAD_DIM)
    bd128 = _block_diag_ones(LANES, HEAD_DIM)

    for l in range(depth):
        q, k, v, zg, xbc, dt, hq, hlf, hk, hv, hg = _in_proj(
            h, norm_mix_w[l][None, :], _pack_w_in(w_in[l]),
            jnp.tile(sb_q_norm[l], SB_HEADS)[None, :], jnp.tile(sb_k_norm[l], SB_HEADS)[None, :], bd256,
            _pad_lanes(ssd_dt_bias[l], DT_W), lbs[l][None, :], length)
        o_sb = _sb_attn(q, k, v, sb_out_norm[l].reshape(1, SB_W), usum, bd128, bsz, length)
        o_ssd = _ssd(zg, xbc, dt, ssd_conv_w[l], ssd_conv_b[l][None, :],
                     _pad_lanes(ssd_A_log[l], DT_W), jnp.repeat(ssd_D[l].astype(F32), HEAD_DIM)[None, :],
                     ssd_norm_w[l].reshape(1, SSD_W), ltri, eexp, shift, bsz, length)
        o_hg = _hgrn2(hq, hlf, hk, hv, hg, hg_out_norm[l].reshape(1, HG_W), ltri2, bd256, bsz, length)
        h = _out_mlp(o_sb, o_ssd, o_hg, h, w_out[l].astype(BF16), norm_mlp_w[l][None, :],
                     w_up[l].astype(BF16), w_down[l].astype(BF16))
    return h.reshape(bsz, length, d)[:, CHUNK:]
```

```python
import functools

import numpy as np
import jax
import jax.numpy as jnp
from jax import lax
from jax.experimental import pallas as pl
from jax.experimental.pallas import tpu as pltpu

F32 = jnp.float32
BF16 = jnp.bfloat16

N_META = 16
CHUNK = 128
PAD = CHUNK - N_META
HEAD_DIM = 64
SB_HEADS = 4
SB_W = SB_HEADS * HEAD_DIM
SSD_HEADS = 8
SSD_W = SSD_HEADS * HEAD_DIM
SSD_GROUPS = 2
SSD_STATE = 128
SSD_CONV = 4
SSD_BC_W = SSD_GROUPS * SSD_STATE
SSD_CONV_DIM = SSD_W + 2 * SSD_BC_W
HG_HEADS = 4
HG_DK = 64
HG_W = HG_HEADS * HG_DK
EPS = 1e-6
TINY = 1e-30
LOG2E = 1.4426950408889634
HG_SAFE_SPAN = 60.0
SUB = 16
SB_BLOCK = 3 * CHUNK
SB_DEAD_MASS = 150.0
LANES = 128
DT_W = LANES
FF_BLOCK = 1024
VMEM_CAP_V7X = 64 * 1024 * 1024

_C_Q, _C_K, _C_V = 0, SB_W, 2 * SB_W
_C_Z = 3 * SB_W
_C_XBC = _C_Z + SSD_W
_C_DT = _C_XBC + SSD_CONV_DIM
_C_HQ = _C_DT + DT_W
_C_HF = _C_HQ + HG_W
_C_HI = _C_HF + HG_W
_C_HG = _C_HI + HG_W
D_IN_PACKED = _C_HG + HG_W


def _vmem_limit(need_bytes):
    return int(min(max(need_bytes, 32 * 1024 * 1024), VMEM_CAP_V7X - 6 * 1024 * 1024))


def _seqs_per_step(bsz):
    return next(n for n in (4, 2, 1) if bsz % n == 0)


def _row_tile(rows, target):
    t = min(target, rows)
    while rows % t:
        t -= CHUNK
    return t


def _dot(a, b):
    return jnp.dot(a, b, preferred_element_type=F32)


def _dot_nt(a, b):
    return lax.dot_general(a, b, (((1,), (1,)), ((), ())), preferred_element_type=F32)


def _dot_tn(a, b):
    return lax.dot_general(a, b, (((0,), (0,)), ((), ())), preferred_element_type=F32)


def _split2(x):
    hi = x.astype(BF16)
    lo = (x - hi.astype(F32)).astype(BF16)
    return hi, lo


def _split3(x):
    hi = x.astype(BF16)
    r = x - hi.astype(F32)
    mid = r.astype(BF16)
    lo = (r - mid.astype(F32)).astype(BF16)
    return hi, mid, lo


def _dot_f32_right(x, m):
    hi, lo = _split2(x)
    return _dot(hi, m) + _dot(lo, m)


def _dot_f32_left3(m, x):
    hi, mid, lo = _split3(x)
    return _dot(m, hi) + _dot(m, mid) + _dot(m, lo)


def _softplus(x):
    return jnp.maximum(x, 0.0) + jnp.log(1.0 + jnp.exp(-jnp.abs(x)))


def _sigmoid(x):
    return 1.0 / (1.0 + jnp.exp(-x))


def _silu(x):
    return x * _sigmoid(x)


def _in_proj_body(tiles_per_seq, from_input, *refs):
    if from_input:
        n_views = len(refs) - 20
        lead_ref, view_refs, refs = refs[0], refs[1:1 + n_views], refs[1 + n_views:]
        h_out_ref = refs[-1]
    else:
        h_ref, refs = refs[0], refs[1:]
    (nw_ref, w_ref, qn_ref, kn_ref, bd_ref, dtb_ref, lb_ref,
     q_ref, k_ref, v_ref, zg_ref, xbc_ref, dt_ref, hq_ref, hlf_ref, hk_ref, hv_ref, hg_ref) = refs[:18]
    tile = pl.program_id(0) % tiles_per_seq
    if from_input:
        first = jnp.where(tile == 0, lead_ref[...], view_refs[0][...])
        x = jnp.concatenate([first] + [r[...] for r in view_refs[1:]], axis=0)
        h_out_ref[...] = x
    else:
        x = h_ref[...]
    tm = x.shape[0]
    rowi = lax.broadcasted_iota(jnp.int32, (tm, 1), 0)
    valid = tile * tm + rowi >= PAD
    ms = jnp.mean(x * x, axis=-1, keepdims=True)
    hn = (x * lax.rsqrt(ms + EPS) * nw_ref[...]).astype(BF16)

    def seg(lo, width):
        return _dot(hn, w_ref[:, lo:lo + width])

    def head_norm(t, w):
        hms = _dot_f32_right(t * t, bd_ref[...]) * (1.0 / HEAD_DIM)
        return t * lax.rsqrt(hms + EPS) * w

    q = head_norm(seg(_C_Q, SB_W), qn_ref[...])
    q_ref[...] = (q * (HEAD_DIM ** -0.5 * LOG2E)).astype(BF16)
    k_ref[...] = jnp.where(valid, head_norm(seg(_C_K, SB_W), kn_ref[...]), 0.0).astype(BF16)
    v_ref[...] = jnp.where(valid, seg(_C_V, SB_W), 0.0).astype(BF16)
    zg_ref[...] = _silu(seg(_C_Z, SSD_W))
    xbc_ref[...] = seg(_C_XBC, SSD_CONV_DIM)
    dt_ref[...] = _softplus(seg(_C_DT, DT_W) + dtb_ref[...])
    hq_ref[...] = _silu(seg(_C_HQ, HG_W))
    lb = lb_ref[...]
    fl = seg(_C_HF, HG_W)
    hlf_ref[...] = jnp.log(jnp.maximum(lb + (1.0 - lb) * _sigmoid(fl), TINY))
    hk_ref[...] = (1.0 - lb) * _sigmoid(-fl)
    hv_ref[...] = seg(_C_HI, HG_W)
    hg_ref[...] = _silu(seg(_C_HG, HG_W))


def _in_proj(h, norm_w, w_packed, qn, kn, bd256, dt_bias, lb, length, lead=None):
    from_input = lead is not None
    d = h.shape[-1]
    rows = h.shape[0] * length if from_input else h.shape[0]
    tm = _row_tile(length, 512)
    tiles_per_seq = length // tm
    widths = (SB_W, SB_W, SB_W, SSD_W, SSD_CONV_DIM, DT_W, HG_W, HG_W, HG_W, HG_W, HG_W)
    dtypes = (BF16, BF16, BF16, F32, F32, F32, F32, F32, F32, F32, F32)
    const = lambda i: (0, 0)
    if from_input:
        chunks_in = (length - CHUNK) // CHUNK
        per_tile = tm // CHUNK

        def view(m):
            return pl.BlockSpec((None, CHUNK, d), lambda i: (
                (i // tiles_per_seq) * chunks_in + jnp.maximum((i % tiles_per_seq) * per_tile + m - 1, 0), 0, 0))

        chunks = h.reshape(h.shape[0] * chunks_in, CHUNK, d)
        row_args = [lead] + [chunks] * per_tile
        row_specs = [pl.BlockSpec((CHUNK, d), const)] + [view(m) for m in range(per_tile)]
        widths, dtypes = widths + (d,), dtypes + (F32,)
    else:
        row_args = [h]
        row_specs = [pl.BlockSpec((tm, d), lambda i: (i, 0))]
    out_bytes = sum(w * jnp.dtype(t).itemsize for w, t in zip(widths, dtypes)) * tm
    need = 2 * (tm * d * 4 + d * D_IN_PACKED * 2 + out_bytes) + tm * SSD_CONV_DIM * 4 * 4
    return pl.pallas_call(
        functools.partial(_in_proj_body, tiles_per_seq, from_input),
        grid=(rows // tm,),
        in_specs=row_specs + [
            pl.BlockSpec((1, d), const),
            pl.BlockSpec((d, D_IN_PACKED), const),
            pl.BlockSpec((1, SB_W), const),
            pl.BlockSpec((1, SB_W), const),
            pl.BlockSpec((SB_W, SB_W), const),
            pl.BlockSpec((1, DT_W), const),
            pl.BlockSpec((1, HG_W), const),
        ],
        out_specs=[pl.BlockSpec((tm, w), lambda i: (i, 0)) for w in widths],
        out_shape=[jax.ShapeDtypeStruct((rows, w), t) for w, t in zip(widths, dtypes)],
        compiler_params=pltpu.CompilerParams(
            dimension_semantics=("parallel",), vmem_limit_bytes=_vmem_limit(need)),
        name="in_proj",
    )(*row_args, norm_w, w_packed, qn, kn, bd256, dt_bias, lb)


def _sb_body(q_ref, k_ref, v_ref, onw_ref, usum_ref, bd_ref, o_ref, acc_ref, carry_ref):
    ib = pl.program_id(2)
    nsub = SB_BLOCK // CHUNK
    lane = lax.broadcasted_iota(jnp.int32, (SB_BLOCK, LANES), 1)
    q = q_ref[...]
    zero = jnp.zeros_like(q)
    q2 = jnp.concatenate([jnp.where(lane < HEAD_DIM, q, zero), jnp.where(lane >= HEAD_DIM, q, zero)], axis=0)
    acc_ref[...] = jnp.zeros_like(acc_ref)
    carry_ref[...] = jnp.zeros_like(carry_ref)

    def block_rows(jb):
        return pl.ds(pl.multiple_of(jb * SB_BLOCK, SB_BLOCK), SB_BLOCK)

    def logits(jb):
        return _dot_nt(q2, k_ref[block_rows(jb), :])

    def accumulate(w, jb):
        acc_ref[...] += _dot(w, v_ref[block_rows(jb), :])

    def weights(z, mask):
        sp = jnp.maximum(z, 0.0) + jnp.log2(1.0 + jnp.exp2(-jnp.abs(z)))
        drop = sp if mask is None else jnp.where(mask, sp, 0.0)
        drop = drop.astype(BF16)
        carry = carry_ref[...]
        ws = [None] * nsub
        for m in reversed(range(nsub)):
            cols = slice(m * CHUNK, (m + 1) * CHUNK)
            sums = _dot(drop[:, cols], usum_ref[...])
            log_w = (z[:, cols] - sp[:, cols]) - sums[:, :CHUNK] - carry
            carry = carry + sums[:, CHUNK:]
            if mask is None:
                w = jnp.exp2(log_w)
            else:
                w = jnp.where(mask[:, cols], jnp.exp2(jnp.where(mask[:, cols], log_w, 0.0)), 0.0)
            ws[m] = w.astype(BF16)
        carry_ref[...] = carry
        return jnp.concatenate(ws, axis=1)

    row = lax.broadcasted_iota(jnp.int32, (SB_BLOCK, SB_BLOCK), 0)
    col = lax.broadcasted_iota(jnp.int32, (SB_BLOCK, SB_BLOCK), 1)
    causal = col < row
    accumulate(weights(logits(ib), jnp.concatenate([causal, causal], axis=0)), ib)

    def live():
        return (jnp.min(carry_ref[...]) < SB_DEAD_MASS).astype(jnp.int32)

    def cond(state):
        jb, alive = state
        return jnp.logical_and(jb >= 0, alive > 0)

    def body(state):
        jb, _ = state
        accumulate(weights(logits(jb), None), jb)
        return jb - 1, live()

    lax.while_loop(cond, body, (ib - 1, live()))

    o = jnp.where(lane < HEAD_DIM, acc_ref[0:SB_BLOCK, :], acc_ref[SB_BLOCK:, :])
    hms = _dot_f32_right(o * o, bd_ref[...]) * (1.0 / HEAD_DIM)
    o_ref[...] = (o * lax.rsqrt(hms + EPS) * onw_ref[...]).astype(BF16)


def _sb_attn(q, k, v, out_norm, usum, bd128, bsz, length):
    assert length % SB_BLOCK == 0
    nb = length // SB_BLOCK
    const = lambda b, p, i: (0, 0)
    need = 2 * (2 * length * LANES * 2) + 24 * 2 * SB_BLOCK * SB_BLOCK * 4
    return pl.pallas_call(
        _sb_body,
        grid=(bsz, SB_W // LANES, nb),
        in_specs=[
            pl.BlockSpec((SB_BLOCK, LANES), lambda b, p, i: (b * nb + i, p)),
            pl.BlockSpec((length, LANES), lambda b, p, i: (b, p)),
            pl.BlockSpec((length, LANES), lambda b, p, i: (b, p)),
            pl.BlockSpec((1, LANES), lambda b, p, i: (0, p)),
            pl.BlockSpec((CHUNK, 2 * CHUNK), const),
            pl.BlockSpec((LANES, LANES), const),
        ],
        out_specs=pl.BlockSpec((SB_BLOCK, LANES), lambda b, p, i: (b * nb + i, p)),
        out_shape=jax.ShapeDtypeStruct((bsz * length, SB_W), BF16),
        scratch_shapes=[pltpu.VMEM((2 * SB_BLOCK, LANES), F32), pltpu.VMEM((2 * SB_BLOCK, CHUNK), F32)],
        compiler_params=pltpu.CompilerParams(
            dimension_semantics=("parallel", "parallel", "arbitrary"), vmem_limit_bytes=_vmem_limit(need)),
        name="sb_attn",
    )(q, k, v, out_norm, usum, bd128)


def _ssd_body(zg_ref, xbc_ref, dt_ref, cw_ref, cb_ref, alog_ref, dexp_ref, nw_ref, ltri_ref, eexp_ref, shift_ref,
              o_ref, tail_ref, st_ref):
    c = pl.program_id(1)

    @pl.when(c == 0)
    def _():
        st_ref[...] = jnp.zeros_like(st_ref)
        tail_ref[...] = jnp.zeros_like(tail_ref)

    for bb in range(zg_ref.shape[0]):
        _ssd_chunk(bb, c, zg_ref, xbc_ref, dt_ref, cw_ref, cb_ref, alog_ref, dexp_ref, nw_ref, ltri_ref, eexp_ref,
                   shift_ref, o_ref, tail_ref, st_ref)


def _ssd_chunk(bb, c, zg_ref, xbc_ref, dt_ref, cw_ref, cb_ref, alog_ref, dexp_ref, nw_ref, ltri_ref, eexp_ref,
               shift_ref, o_ref, tail_ref, st_ref):
    heads_per_group = SSD_HEADS // SSD_GROUPS
    group_w = SSD_W // SSD_GROUPS
    taps = SSD_CONV - 1

    rowi = lax.broadcasted_iota(jnp.int32, (CHUNK, 1), 0)
    valid = (c * CHUNK + rowi) >= PAD
    u = jnp.where(valid, xbc_ref[bb], 0.0)
    u_hi, u_lo = _split2(u)
    shifted = _dot(shift_ref[...], jnp.concatenate([u_hi, u_lo], axis=0))
    conv = cb_ref[...] + cw_ref[taps:taps + 1, :] * u
    for kk in range(1, taps + 1):
        conv = conv + cw_ref[taps - kk:taps - kk + 1, :] * shifted[(kk - 1) * CHUNK:kk * CHUNK]
    tail = tail_ref[bb]
    row8 = lax.broadcasted_iota(jnp.int32, (8, 1), 0)
    head_fix = jnp.zeros((8, SSD_CONV_DIM), F32)
    for kk in range(1, taps + 1):
        head_fix = head_fix + cw_ref[taps - kk:taps - kk + 1, :] * jnp.where(row8 < kk, pltpu.roll(tail, kk, 0), 0.0)
    conv = jnp.concatenate([conv[:8] + head_fix, conv[8:]], axis=0)
    tail_ref[bb] = u[CHUNK - 8:]
    act = _silu(conv)
    xs = act[:, :SSD_W]
    bm = act[:, SSD_W:SSD_W + SSD_BC_W].astype(BF16)
    cm = act[:, SSD_W + SSD_BC_W:].astype(BF16)

    dt = jnp.where(valid, dt_ref[bb], 0.0)
    a = dt * (-jnp.exp(alog_ref[...]))
    acum = _dot_f32_left3(ltri_ref[...], a)
    acum_t = acum.T
    a_last = acum[CHUNK - 1:CHUNK, :]
    per_head = jnp.concatenate([dt, jnp.exp(acum), jnp.exp(a_last - acum)], axis=0)
    expanded = _dot_f32_right(per_head, eexp_ref[...])
    dt_e = expanded[:CHUNK]
    decay_in_e = expanded[CHUNK:2 * CHUNK]
    decay_out_e = expanded[2 * CHUNK:]
    xdt = xs * dt_e
    xdt_b = xdt.astype(BF16)
    xw_b = (xdt * decay_out_e).astype(BF16)

    row = lax.broadcasted_iota(jnp.int32, (CHUNK, CHUNK), 0)
    col = lax.broadcasted_iota(jnp.int32, (CHUNK, CHUNK), 1)
    causal = row >= col
    lane = lax.broadcasted_iota(jnp.int32, (CHUNK, LANES), 1)
    upper_half = lane >= HEAD_DIM

    ys = []
    for g in range(SSD_GROUPS):
        cg = cm[:, g * SSD_STATE:(g + 1) * SSD_STATE]
        bg = bm[:, g * SSD_STATE:(g + 1) * SSD_STATE]
        gcols = slice(g * group_w, (g + 1) * group_w)
        cb = _dot_nt(cg, bg)
        st = st_ref[bb, g]
        y_off = _dot(cg, st.astype(BF16)) * decay_in_e[:, gcols]
        pairs = []
        for pr in range(heads_per_group // 2):
            xp = xdt_b[:, g * group_w + pr * LANES:g * group_w + (pr + 1) * LANES]
            acc = None
            for hh in range(2):
                h = g * heads_per_group + pr * 2 + hh
                seg = acum[:, h:h + 1] - acum_t[h:h + 1, :]
                decay = jnp.where(causal, jnp.exp(jnp.where(causal, seg, 0.0)), 0.0)
                m = (cb * decay).astype(BF16)
                keep = upper_half if hh else jnp.logical_not(upper_half)
                t = _dot(m, jnp.where(keep, xp, jnp.zeros_like(xp)))
                acc = t if acc is None else acc + t
            pairs.append(acc)
        ys.append(jnp.concatenate(pairs, axis=1) + y_off)
        st_ref[bb, g] = st * decay_in_e[CHUNK - 1:CHUNK, gcols] + _dot_tn(bg, xw_b[:, gcols])

    y = jnp.concatenate(ys, axis=1) + xs * dexp_ref[...]
    y = y * zg_ref[bb]
    outs = []
    for g in range(SSD_GROUPS):
        yg = y[:, g * group_w:(g + 1) * group_w]
        gms = jnp.mean(yg * yg, axis=-1, keepdims=True)
        outs.append(yg * lax.rsqrt(gms + EPS) * nw_ref[:, g * group_w:(g + 1) * group_w])
    o_ref[bb] = jnp.concatenate(outs, axis=1).astype(BF16)


def _ssd(zg, xbc, dt, conv_w, conv_b, a_log, d_exp, norm_w, ltri, eexp, shift, bsz, length):
    nc = length // CHUNK
    per = _seqs_per_step(bsz)
    const = lambda b, c: (0, 0)
    rows = lambda b, c: (b, c, 0)
    group_w = SSD_W // SSD_GROUPS
    need = per * (2 * CHUNK * (SSD_W + SSD_CONV_DIM + DT_W) * 4 * 2 + 64 * CHUNK * SSD_CONV_DIM * 4)
    out = pl.pallas_call(
        _ssd_body,
        grid=(bsz // per, nc),
        in_specs=[
            pl.BlockSpec((per, CHUNK, SSD_W), rows),
            pl.BlockSpec((per, CHUNK, SSD_CONV_DIM), rows),
            pl.BlockSpec((per, CHUNK, DT_W), rows),
            pl.BlockSpec((SSD_CONV, SSD_CONV_DIM), const),
            pl.BlockSpec((1, SSD_CONV_DIM), const),
            pl.BlockSpec((1, DT_W), const),
            pl.BlockSpec((1, SSD_W), const),
            pl.BlockSpec((1, SSD_W), const),
            pl.BlockSpec((CHUNK, CHUNK), const),
            pl.BlockSpec((DT_W, SSD_W), const),
            pl.BlockSpec(((SSD_CONV - 1) * CHUNK, 2 * CHUNK), const),
        ],
        out_specs=pl.BlockSpec((per, CHUNK, SSD_W), rows),
        out_shape=jax.ShapeDtypeStruct((bsz, length, SSD_W), BF16),
        scratch_shapes=[pltpu.VMEM((per, 8, SSD_CONV_DIM), F32),
                        pltpu.VMEM((per, SSD_GROUPS, SSD_STATE, group_w), F32)],
        compiler_params=pltpu.CompilerParams(
            dimension_semantics=("parallel", "arbitrary"), vmem_limit_bytes=_vmem_limit(need)),
        name="ssd",
    )(zg.reshape(bsz, length, SSD_W), xbc.reshape(bsz, length, SSD_CONV_DIM), dt.reshape(bsz, length, DT_W),
      conv_w, conv_b, a_log, d_exp, norm_w, ltri, eexp, shift)
    return out.reshape(bsz * length, SSD_W)


def _hg_body(q_ref, lf_ref, k_ref, v_ref, g_ref, onw_ref, ltri2_ref, bd_ref,
             o_ref, st_ref, qs_ref, ks_ref, vs_ref, gs_ref, oacc_ref):
    c = pl.program_id(1)
    nsub = CHUNK // SUB
    per = q_ref.shape[0]

    @pl.when(c == 0)
    def _():
        st_ref[...] = jnp.zeros_like(st_ref)

    rowi = lax.broadcasted_iota(jnp.int32, (CHUNK, 1), 0)
    valid = (c * CHUNK + rowi) >= PAD
    srow = lax.broadcasted_iota(jnp.int32, (HG_W, HG_W), 0) // HG_DK
    scol = lax.broadcasted_iota(jnp.int32, (HG_W, HG_W), 1) // HG_DK
    erow = lax.broadcasted_iota(jnp.int32, (HG_HEADS * SUB, HG_W), 0) // SUB
    ecol = lax.broadcasted_iota(jnp.int32, (HG_HEADS * SUB, HG_W), 1) // HG_DK
    same_head = erow == ecol

    seqs = []
    for bb in range(per):
        log_f = jnp.where(valid, lf_ref[bb], 0.0)
        k = jnp.where(valid, k_ref[bb], 0.0)
        v = jnp.where(valid, v_ref[bb], 0.0)
        q = q_ref[bb]
        cums = _dot_f32_left3(ltri2_ref[...], log_f)
        gc = cums[:CHUNK]
        span = -cums[CHUNK:]
        g_last = gc[CHUNK - 1:CHUNK, :]

        st = st_ref[bb]
        oacc_ref[bb] = _dot_nt((q * jnp.exp(gc)).astype(BF16), st.astype(BF16))
        k_end = (k * jnp.exp(g_last - gc)).astype(BF16)
        v_b = v.astype(BF16)
        st_ref[bb] = st * jnp.exp(g_last) + jnp.where(srow == scol, _dot_tn(v_b, k_end), 0.0)

        for jb in range(nsub - 1):
            r0, r1 = jb * SUB, (jb + 1) * SUB
            g_end = gc[r1 - 1:r1, :]
            ke = k[r0:r1] * jnp.exp(g_end - gc[r0:r1])
            ke4 = jnp.where(same_head, jnp.concatenate([ke] * HG_HEADS, axis=0), 0.0).astype(BF16)
            v4 = jnp.where(same_head, jnp.concatenate([v[r0:r1]] * HG_HEADS, axis=0), 0.0).astype(BF16)
            qp = (q[r1:] * jnp.exp(gc[r1:] - g_end)).astype(BF16)
            scores = _dot_nt(qp, ke4)
            oacc_ref[bb, r1:, :] += _dot(scores.astype(BF16), v4)
        seqs.append((q, k, v, v_b, gc, span))

    widest = seqs[0][5]
    for seq in seqs[1:]:
        widest = jnp.maximum(widest, seq[5])
    safe = jnp.max(widest) < HG_SAFE_SPAN

    @pl.when(safe)
    def _():
        trow = lax.broadcasted_iota(jnp.int32, (CHUNK, CHUNK), 0)
        tcol = lax.broadcasted_iota(jnp.int32, (CHUNK, CHUNK), 1)
        pair = (trow // SUB == tcol // SUB) & (tcol <= trow)
        lane_head = lax.broadcasted_iota(jnp.int32, (CHUNK, HG_W), 1) // HG_DK
        for bb, (q, k, v, v_b, gc, span) in enumerate(seqs):
            qd = (q * jnp.exp(-span)).astype(BF16)
            kd = (k * jnp.exp(span)).astype(BF16)
            acc = jnp.zeros((CHUNK, HG_W), F32)
            for h in range(HG_HEADS):
                of_head = lane_head == h
                sc = _dot_nt(jnp.where(of_head, qd, jnp.zeros_like(qd)), kd)
                acc = acc + _dot(jnp.where(pair, sc, 0.0).astype(BF16), jnp.where(of_head, v_b, jnp.zeros_like(v_b)))
            oacc_ref[bb] += acc

    @pl.when(jnp.logical_not(safe))
    def _():
        rin = lax.broadcasted_iota(jnp.int32, (SUB, 1), 0)
        for bb, (q, k, v, v_b, gc, span) in enumerate(seqs):
            qs_ref[...] = q
            ks_ref[...] = k
            vs_ref[...] = v
            gs_ref[...] = gc

            def diag_block(ib, carry, bb=bb):
                r0 = pl.multiple_of(ib * SUB, SUB)
                qi = qs_ref[pl.ds(r0, SUB), :]
                gi = gs_ref[pl.ds(r0, SUB), :]
                prods = []
                for j in range(SUB):
                    kj = ks_ref[pl.ds(r0 + j, 1), :]
                    gj = gs_ref[pl.ds(r0 + j, 1), :]
                    m = rin >= j
                    prods.append(jnp.where(m, qi * kj * jnp.exp(jnp.where(m, gi - gj, 0.0)), 0.0))
                head_sums = _dot_f32_right(jnp.concatenate(prods, axis=0), bd_ref[...])
                oi = jnp.zeros((SUB, HG_W), F32)
                for j in range(SUB):
                    oi = oi + head_sums[j * SUB:(j + 1) * SUB] * vs_ref[pl.ds(r0 + j, 1), :]
                oacc_ref[bb, pl.ds(r0, SUB), :] += oi
                return carry

            lax.fori_loop(0, nsub, diag_block, 0)

    for bb in range(per):
        o = oacc_ref[bb]
        hms = _dot_f32_right(o * o, bd_ref[...]) * (1.0 / HG_DK)
        o_ref[bb] = (o * lax.rsqrt(hms + EPS) * onw_ref[...] * g_ref[bb]).astype(BF16)


def _hgrn2(hq, hlf, hk, hv, hg, out_norm, ltri2, bd256, bsz, length):
    nc = length // CHUNK
    per = _seqs_per_step(bsz)
    const = lambda b, c: (0, 0)
    rows = lambda b, c: (b, c, 0)
    need = per * 64 * CHUNK * HG_W * 4
    seq3 = lambda t: t.reshape(bsz, length, HG_W)
    out = pl.pallas_call(
        _hg_body,
        grid=(bsz // per, nc),
        in_specs=[pl.BlockSpec((per, CHUNK, HG_W), rows)] * 5 + [
            pl.BlockSpec((1, HG_W), const),
            pl.BlockSpec((2 * CHUNK, CHUNK), const),
            pl.BlockSpec((HG_W, HG_W), const),
        ],
        out_specs=pl.BlockSpec((per, CHUNK, HG_W), rows),
        out_shape=jax.ShapeDtypeStruct((bsz, length, HG_W), BF16),
        scratch_shapes=[pltpu.VMEM((per, HG_W, HG_W), F32)] + [pltpu.VMEM((CHUNK, HG_W), F32)] * 4
        + [pltpu.VMEM((per, CHUNK, HG_W), F32)],
        compiler_params=pltpu.CompilerParams(
            dimension_semantics=("parallel", "arbitrary"), vmem_limit_bytes=_vmem_limit(need)),
        name="hgrn2",
    )(seq3(hq), seq3(hlf), seq3(hk), seq3(hv), seq3(hg), out_norm, ltri2, bd256)
    return out.reshape(bsz * length, HG_W)


def _out_mlp_body(n_views, *refs):
    wo_ref, nw_ref, wup_ref, wdn_ref, out_ref = refs[4 * n_views:]

    def rows(k):
        return jnp.concatenate([r[...] for r in refs[k * n_views:(k + 1) * n_views]], axis=0)

    h1 = (rows(3)
          + _dot(rows(0), wo_ref[0:SB_W, :])
          + _dot(rows(1), wo_ref[SB_W:SB_W + SSD_W, :])
          + _dot(rows(2), wo_ref[SB_W + SSD_W:, :]))
    ms = jnp.mean(h1 * h1, axis=-1, keepdims=True)
    hn = (h1 * lax.rsqrt(ms + EPS) * nw_ref[...]).astype(BF16)
    mlp = jnp.zeros_like(h1)
    for c in range(wup_ref.shape[1] // FF_BLOCK):
        u = _dot(hn, wup_ref[:, c * FF_BLOCK:(c + 1) * FF_BLOCK])
        act = jnp.square(jnp.maximum(u, 0.0)).astype(BF16)
        mlp = mlp + _dot(act, wdn_ref[c * FF_BLOCK:(c + 1) * FF_BLOCK, :])
    out_ref[...] = h1 + mlp


def _out_mlp(o_sb, o_ssd, o_hg, h, w_out, norm_w, w_up, w_down, drop_lead=None):
    rows, d = h.shape
    d_ff = w_up.shape[1]
    const = lambda i: (0, 0)
    operands = (o_sb, o_ssd, o_hg, h)
    if drop_lead is None:
        tm = _row_tile(rows, 512)
        n_views, grid = 1, rows // tm
        row_args = list(operands)
        row_specs = [pl.BlockSpec((tm, t.shape[1]), lambda i: (i, 0)) for t in operands]
        out_spec = pl.BlockSpec((tm, d), lambda i: (i, 0))
        out_shape = jax.ShapeDtypeStruct((rows, d), F32)
    else:
        bsz, length = drop_lead
        chunks_seq = length // CHUNK
        tm = _row_tile(length, 512)
        n_views = tm // CHUNK
        tiles_per_seq = pl.cdiv(length - CHUNK, tm)

        def view(width, m):
            return pl.BlockSpec((None, CHUNK, width), lambda i: (
                (i // tiles_per_seq) * chunks_seq
                + jnp.minimum(1 + (i % tiles_per_seq) * n_views + m, chunks_seq - 1), 0, 0))

        grid = bsz * tiles_per_seq
        row_args = [t.reshape(rows // CHUNK, CHUNK, t.shape[1]) for t in operands for _ in range(n_views)]
        row_specs = [view(t.shape[1], m) for t in operands for m in range(n_views)]
        out_spec = pl.BlockSpec((None, tm, d), lambda i: (i // tiles_per_seq, i % tiles_per_seq, 0))
        out_shape = jax.ShapeDtypeStruct((bsz, length - CHUNK, d), F32)
    weights = (w_out.shape[0] * d + 2 * d * d_ff) * 2
    need = 2 * weights + 2 * tm * (2 * d * 4 + (SB_W + SSD_W + HG_W) * 2) + 6 * tm * FF_BLOCK * 4
    return pl.pallas_call(
        functools.partial(_out_mlp_body, n_views),
        grid=(grid,),
        in_specs=row_specs + [
            pl.BlockSpec(w_out.shape, const),
            pl.BlockSpec((1, d), const),
            pl.BlockSpec(w_up.shape, const),
            pl.BlockSpec(w_down.shape, const),
        ],
        out_specs=out_spec,
        out_shape=out_shape,
        compiler_params=pltpu.CompilerParams(
            dimension_semantics=("parallel",), vmem_limit_bytes=_vmem_limit(need)),
        name="out_mlp",
    )(*row_args, w_out, norm_w, w_up, w_down)


def _block_diag_ones(n, block):
    idx = np.arange(n) // block
    return jnp.asarray(idx[:, None] == idx[None, :], BF16)


def _constants():
    t = np.arange(CHUNK)
    ltri = jnp.asarray(t[None, :] <= t[:, None], BF16)
    later = (t[:, None] > t[None, :])
    usum = jnp.asarray(np.concatenate([later, np.ones((CHUNK, CHUNK), bool)], axis=1), BF16)
    eexp = np.zeros((DT_W, SSD_W), bool)
    for h in range(SSD_HEADS):
        eexp[h, h * HEAD_DIM:(h + 1) * HEAD_DIM] = True
    in_sub = (t[None, :] <= t[:, None]) & (t[None, :] // SUB == t[:, None] // SUB)
    ltri2 = jnp.asarray(np.concatenate([t[None, :] <= t[:, None], in_sub], axis=0), BF16)
    shift = np.concatenate([t[None, :] == t[:, None] - k for k in range(1, SSD_CONV)], axis=0)
    shift = jnp.asarray(np.concatenate([shift, shift], axis=1), BF16)
    return ltri, ltri2, usum, jnp.asarray(eexp, BF16), shift


def _pack_w_in(w):
    d = w.shape[0]
    dt_lo = 3 * SB_W + 2 * SSD_W + 2 * SSD_BC_W
    dt_hi = dt_lo + SSD_HEADS
    return jnp.concatenate(
        [w[:, :dt_lo], w[:, dt_lo:dt_hi], jnp.zeros((d, DT_W - SSD_HEADS), w.dtype), w[:, dt_hi:]],
        axis=1).astype(BF16)


def _pad_lanes(v, width):
    return jnp.pad(v.astype(F32), (0, width - v.shape[0]))[None, :]


def kernel(x, meta_tokens, hg_lb_logits, norm_mix_w, w_in, sb_q_norm, sb_k_norm, sb_out_norm, ssd_conv_w,
           ssd_conv_b, ssd_dt_bias, ssd_A_log, ssd_D, ssd_norm_w, hg_out_norm, w_out, norm_mlp_w, w_up, w_down):
    bsz, seq, d = x.shape
    depth = w_in.shape[0]
    length = seq + CHUNK
    lead = jnp.concatenate([jnp.zeros((PAD, d), x.dtype), meta_tokens.astype(x.dtype)], axis=0)
    h = x

    probs = jax.nn.softmax(hg_lb_logits.astype(F32), axis=0)
    lbs = jnp.concatenate([jnp.zeros_like(probs[0:1]), jnp.cumsum(probs, axis=0)[:-1]], axis=0)

    ltri, ltri2, usum, eexp, shift = _constants()
    bd256 = _block_diag_ones(SB_W, HEAD_DIM)
    bd128 = _block_diag_ones(LANES, HEAD_DIM)

    for l in range(depth):
        proj = _in_proj(
            h, norm_mix_w[l][None, :], _pack_w_in(w_in[l]),
            jnp.tile(sb_q_norm[l], SB_HEADS)[None, :], jnp.tile(sb_k_norm[l], SB_HEADS)[None, :], bd256,
            _pad_lanes(ssd_dt_bias[l], DT_W), lbs[l][None, :], length, lead if l == 0 else None)
        if l == 0:
            h = proj[-1]
        q, k, v, zg, xbc, dt, hq, hlf, hk, hv, hg = proj[:11]
        o_sb = _sb_attn(q, k, v, sb_out_norm[l].reshape(1, SB_W), usum, bd128, bsz, length)
        o_ssd = _ssd(zg, xbc, dt, ssd_conv_w[l], ssd_conv_b[l][None, :],
                     _pad_lanes(ssd_A_log[l], DT_W), jnp.repeat(ssd_D[l].astype(F32), HEAD_DIM)[None, :],
                     ssd_norm_w[l].reshape(1, SSD_W), ltri, eexp, shift, bsz, length)
        o_hg = _hgrn2(hq, hlf, hk, hv, hg, hg_out_norm[l].reshape(1, HG_W), ltri2, bd256, bsz, length)
        h = _out_mlp(o_sb, o_ssd, o_hg, h, w_out[l].astype(BF16), norm_mlp_w[l][None, :],
                     w_up[l].astype(BF16), w_down[l].astype(BF16), (bsz, length) if l == depth - 1 else None)
    return h
```

```python
import functools

import numpy as np
import jax
import jax.numpy as jnp
from jax import lax
from jax.experimental import pallas as pl
from jax.experimental.pallas import tpu as pltpu

F32 = jnp.float32
BF16 = jnp.bfloat16

N_META = 16
CHUNK = 128
PAD = CHUNK - N_META
HEAD_DIM = 64
SB_HEADS = 4
SB_W = SB_HEADS * HEAD_DIM
SSD_HEADS = 8
SSD_W = SSD_HEADS * HEAD_DIM
SSD_GROUPS = 2
SSD_STATE = 128
SSD_CONV = 4
SSD_BC_W = SSD_GROUPS * SSD_STATE
SSD_CONV_DIM = SSD_W + 2 * SSD_BC_W
HG_HEADS = 4
HG_DK = 64
HG_W = HG_HEADS * HG_DK
EPS = 1e-6
TINY = 1e-30
LOG2E = 1.4426950408889634
HG_SAFE_SPAN = 60.0
SUB = 16
SB_BLOCK = 3 * CHUNK
SB_DEAD_MASS = 150.0
LANES = 128
DT_W = LANES
FF_BLOCK = 1024
VMEM_CAP_V7X = 64 * 1024 * 1024

_C_Q, _C_K, _C_V = 0, SB_W, 2 * SB_W
_C_Z = 3 * SB_W
_C_XBC = _C_Z + SSD_W
_C_DT = _C_XBC + SSD_CONV_DIM
_C_HQ = _C_DT + DT_W
_C_HF = _C_HQ + HG_W
_C_HI = _C_HF + HG_W
_C_HG = _C_HI + HG_W
D_IN_PACKED = _C_HG + HG_W


def _vmem_limit(need_bytes):
    return int(min(max(need_bytes, 32 * 1024 * 1024), VMEM_CAP_V7X - 6 * 1024 * 1024))


def _seqs_per_step(bsz):
    return next(n for n in (4, 2, 1) if bsz % n == 0)


def _row_tile(rows, target):
    t = min(target, rows)
    while rows % t:
        t -= CHUNK
    return t


def _dot(a, b):
    return jnp.dot(a, b, preferred_element_type=F32)


def _dot_nt(a, b):
    return lax.dot_general(a, b, (((1,), (1,)), ((), ())), preferred_element_type=F32)


def _dot_tn(a, b):
    return lax.dot_general(a, b, (((0,), (0,)), ((), ())), preferred_element_type=F32)


def _split2(x):
    hi = x.astype(BF16)
    lo = (x - hi.astype(F32)).astype(BF16)
    return hi, lo


def _split3(x):
    hi = x.astype(BF16)
    r = x - hi.astype(F32)
    mid = r.astype(BF16)
    lo = (r - mid.astype(F32)).astype(BF16)
    return hi, mid, lo


def _dot_f32_right(x, m):
    hi, lo = _split2(x)
    return _dot(hi, m) + _dot(lo, m)


def _dot_f32_left3(m, x):
    hi, mid, lo = _split3(x)
    return _dot(m, hi) + _dot(m, mid) + _dot(m, lo)


def _softplus(x):
    return jnp.maximum(x, 0.0) + jnp.log(1.0 + jnp.exp(-jnp.abs(x)))


def _sigmoid(x):
    return 1.0 / (1.0 + jnp.exp(-x))


def _silu(x):
    return x * _sigmoid(x)


def _in_proj_body(tiles_per_seq, from_input, *refs):
    if from_input:
        n_views = len(refs) - 20
        lead_ref, view_refs, refs = refs[0], refs[1:1 + n_views], refs[1 + n_views:]
        h_out_ref = refs[-1]
    else:
        h_ref, refs = refs[0], refs[1:]
    (nw_ref, w_ref, qn_ref, kn_ref, bd_ref, dtb_ref, lb_ref,
     q_ref, k_ref, v_ref, zg_ref, xbc_ref, dt_ref, hq_ref, hlf_ref, hk_ref, hv_ref, hg_ref) = refs[:18]
    tile = pl.program_id(0) % tiles_per_seq
    if from_input:
        first = jnp.where(tile == 0, lead_ref[...], view_refs[0][...])
        x = jnp.concatenate([first] + [r[...] for r in view_refs[1:]], axis=0)
        h_out_ref[...] = x
    else:
        x = h_ref[...]
    tm = x.shape[0]
    rowi = lax.broadcasted_iota(jnp.int32, (tm, 1), 0)
    valid = tile * tm + rowi >= PAD
    ms = jnp.mean(x * x, axis=-1, keepdims=True)
    hn = (x * lax.rsqrt(ms + EPS) * nw_ref[...]).astype(BF16)

    def seg(lo, width):
        return _dot(hn, w_ref[:, lo:lo + width])

    def head_norm(t, w):
        hms = _dot_f32_right(t * t, bd_ref[...]) * (1.0 / HEAD_DIM)
        return t * lax.rsqrt(hms + EPS) * w

    q = head_norm(seg(_C_Q, SB_W), qn_ref[...])
    q_ref[...] = (q * (HEAD_DIM ** -0.5 * LOG2E)).astype(BF16)
    k_ref[...] = jnp.where(valid, head_norm(seg(_C_K, SB_W), kn_ref[...]), 0.0).astype(BF16)
    v_ref[...] = jnp.where(valid, seg(_C_V, SB_W), 0.0).astype(BF16)
    zg_ref[...] = _silu(seg(_C_Z, SSD_W))
    xbc_ref[...] = seg(_C_XBC, SSD_CONV_DIM)
    dt_ref[...] = _softplus(seg(_C_DT, DT_W) + dtb_ref[...])
    hq_ref[...] = _silu(seg(_C_HQ, HG_W))
    lb = lb_ref[...]
    fl = seg(_C_HF, HG_W)
    hlf_ref[...] = jnp.log(jnp.maximum(lb + (1.0 - lb) * _sigmoid(fl), TINY))
    hk_ref[...] = (1.0 - lb) * _sigmoid(-fl)
    hv_ref[...] = seg(_C_HI, HG_W)
    hg_ref[...] = _silu(seg(_C_HG, HG_W))


def _in_proj(h, norm_w, w_packed, qn, kn, bd256, dt_bias, lb, length, lead=None):
    from_input = lead is not None
    d = h.shape[-1]
    rows = h.shape[0] * length if from_input else h.shape[0]
    tm = _row_tile(length, 512)
    tiles_per_seq = length // tm
    widths = (SB_W, SB_W, SB_W, SSD_W, SSD_CONV_DIM, DT_W, HG_W, HG_W, HG_W, HG_W, HG_W)
    dtypes = (BF16, BF16, BF16, F32, F32, F32, F32, F32, F32, F32, F32)
    const = lambda i: (0, 0)
    if from_input:
        chunks_in = (length - CHUNK) // CHUNK
        per_tile = tm // CHUNK

        def view(m):
            return pl.BlockSpec((None, CHUNK, d), lambda i: (
                (i // tiles_per_seq) * chunks_in + jnp.maximum((i % tiles_per_seq) * per_tile + m - 1, 0), 0, 0))

        chunks = h.reshape(h.shape[0] * chunks_in, CHUNK, d)
        row_args = [lead] + [chunks] * per_tile
        row_specs = [pl.BlockSpec((CHUNK, d), const)] + [view(m) for m in range(per_tile)]
        widths, dtypes = widths + (d,), dtypes + (F32,)
    else:
        row_args = [h]
        row_specs = [pl.BlockSpec((tm, d), lambda i: (i, 0))]
    out_bytes = sum(w * jnp.dtype(t).itemsize for w, t in zip(widths, dtypes)) * tm
    need = 2 * (tm * d * 4 + d * D_IN_PACKED * 2 + out_bytes) + tm * SSD_CONV_DIM * 4 * 4
    return pl.pallas_call(
        functools.partial(_in_proj_body, tiles_per_seq, from_input),
        grid=(rows // tm,),
        in_specs=row_specs + [
            pl.BlockSpec((1, d), const),
            pl.BlockSpec((d, D_IN_PACKED), const),
            pl.BlockSpec((1, SB_W), const),
            pl.BlockSpec((1, SB_W), const),
            pl.BlockSpec((SB_W, SB_W), const),
            pl.BlockSpec((1, DT_W), const),
            pl.BlockSpec((1, HG_W), const),
        ],
        out_specs=[pl.BlockSpec((tm, w), lambda i: (i, 0)) for w in widths],
        out_shape=[jax.ShapeDtypeStruct((rows, w), t) for w, t in zip(widths, dtypes)],
        compiler_params=pltpu.CompilerParams(
            dimension_semantics=("parallel",), vmem_limit_bytes=_vmem_limit(need)),
        name="in_proj",
    )(*row_args, norm_w, w_packed, qn, kn, bd256, dt_bias, lb)


def _sb_body(q_ref, k_ref, v_ref, onw_ref, usum_ref, bd_ref, o_ref, acc_ref, carry_ref):
    ib = pl.program_id(2)
    nsub = SB_BLOCK // CHUNK
    lane = lax.broadcasted_iota(jnp.int32, (SB_BLOCK, LANES), 1)
    q = q_ref[...]
    zero = jnp.zeros_like(q)
    q2 = jnp.concatenate([jnp.where(lane < HEAD_DIM, q, zero), jnp.where(lane >= HEAD_DIM, q, zero)], axis=0)
    acc_ref[...] = jnp.zeros_like(acc_ref)
    carry_ref[...] = jnp.zeros_like(carry_ref)

    def block_rows(jb):
        return pl.ds(pl.multiple_of(jb * SB_BLOCK, SB_BLOCK), SB_BLOCK)

    def logits(jb):
        return _dot_nt(q2, k_ref[block_rows(jb), :])

    def accumulate(w, jb):
        acc_ref[...] += _dot(w, v_ref[block_rows(jb), :])

    def drop_mass(z):
        return jnp.maximum(z, 0.0) + jnp.log2(1.0 + jnp.exp2(-jnp.abs(z)))

    def weights(z, mask):
        sp = drop_mass(z)
        drop = (sp if mask is None else jnp.where(mask, sp, 0.0)).astype(BF16)
        carry = carry_ref[...]
        ws = [None] * nsub
        for m in reversed(range(nsub)):
            cols = slice(m * CHUNK, (m + 1) * CHUNK)
            sums = _dot(drop[:, cols], usum_ref[...])
            log_w = (z[:, cols] - sp[:, cols]) - sums[:, :CHUNK] - carry
            carry = carry + sums[:, CHUNK:]
            if mask is None:
                w = jnp.exp2(log_w)
            else:
                w = jnp.where(mask[:, cols], jnp.exp2(jnp.where(mask[:, cols], log_w, 0.0)), 0.0)
            ws[m] = w.astype(BF16)
        carry_ref[...] = carry
        return jnp.concatenate(ws, axis=1)

    row = lax.broadcasted_iota(jnp.int32, (SB_BLOCK, SB_BLOCK), 0)
    col = lax.broadcasted_iota(jnp.int32, (SB_BLOCK, SB_BLOCK), 1)
    causal = col < row
    accumulate(weights(logits(ib), jnp.concatenate([causal, causal], axis=0)), ib)

    def live():
        return (jnp.min(carry_ref[...]) < SB_DEAD_MASS).astype(jnp.int32)

    def cond(state):
        jb, alive = state
        return jnp.logical_and(jb >= 0, alive > 0)

    def body(state):
        jb, _ = state
        accumulate(weights(logits(jb), None), jb)
        return jb - 1, live()

    lax.while_loop(cond, body, (ib - 1, live()))

    o = jnp.where(lane < HEAD_DIM, acc_ref[0:SB_BLOCK, :], acc_ref[SB_BLOCK:, :])
    hms = _dot_f32_right(o * o, bd_ref[...]) * (1.0 / HEAD_DIM)
    o_ref[...] = (o * lax.rsqrt(hms + EPS) * onw_ref[...]).astype(BF16)


def _sb_attn(q, k, v, out_norm, usum, bd128, bsz, length):
    assert length % SB_BLOCK == 0
    nb = length // SB_BLOCK
    const = lambda b, p, i: (0, 0)
    need = 2 * (2 * length * LANES * 2) + 24 * 2 * SB_BLOCK * SB_BLOCK * 4
    return pl.pallas_call(
        _sb_body,
        grid=(bsz, SB_W // LANES, nb),
        in_specs=[
            pl.BlockSpec((SB_BLOCK, LANES), lambda b, p, i: (b * nb + i, p)),
            pl.BlockSpec((length, LANES), lambda b, p, i: (b, p)),
            pl.BlockSpec((length, LANES), lambda b, p, i: (b, p)),
            pl.BlockSpec((1, LANES), lambda b, p, i: (0, p)),
            pl.BlockSpec((CHUNK, 2 * CHUNK), const),
            pl.BlockSpec((LANES, LANES), const),
        ],
        out_specs=pl.BlockSpec((SB_BLOCK, LANES), lambda b, p, i: (b * nb + i, p)),
        out_shape=jax.ShapeDtypeStruct((bsz * length, SB_W), BF16),
        scratch_shapes=[pltpu.VMEM((2 * SB_BLOCK, LANES), F32), pltpu.VMEM((2 * SB_BLOCK, CHUNK), F32)],
        compiler_params=pltpu.CompilerParams(
            dimension_semantics=("parallel", "parallel", "arbitrary"), vmem_limit_bytes=_vmem_limit(need)),
        name="sb_attn",
    )(q, k, v, out_norm, usum, bd128)


def _ssd_body(zg_ref, xbc_ref, dt_ref, cw_ref, cb_ref, alog_ref, dexp_ref, nw_ref, ltri_ref, eexp_ref, shift_ref,
              o_ref, tail_ref, st_ref):
    c = pl.program_id(1)

    @pl.when(c == 0)
    def _():
        st_ref[...] = jnp.zeros_like(st_ref)
        tail_ref[...] = jnp.zeros_like(tail_ref)

    for bb in range(zg_ref.shape[0]):
        _ssd_chunk(bb, c, zg_ref, xbc_ref, dt_ref, cw_ref, cb_ref, alog_ref, dexp_ref, nw_ref, ltri_ref, eexp_ref,
                   shift_ref, o_ref, tail_ref, st_ref)


def _ssd_chunk(bb, c, zg_ref, xbc_ref, dt_ref, cw_ref, cb_ref, alog_ref, dexp_ref, nw_ref, ltri_ref, eexp_ref,
               shift_ref, o_ref, tail_ref, st_ref):
    heads_per_group = SSD_HEADS // SSD_GROUPS
    group_w = SSD_W // SSD_GROUPS
    taps = SSD_CONV - 1

    rowi = lax.broadcasted_iota(jnp.int32, (CHUNK, 1), 0)
    valid = (c * CHUNK + rowi) >= PAD
    u = jnp.where(valid, xbc_ref[bb], 0.0)
    shifted = _dot(shift_ref[...], u.astype(BF16))
    conv = cb_ref[...] + cw_ref[taps:taps + 1, :] * u
    for kk in range(1, taps + 1):
        conv = conv + cw_ref[taps - kk:taps - kk + 1, :] * shifted[(kk - 1) * CHUNK:kk * CHUNK]
    tail = tail_ref[bb]
    row8 = lax.broadcasted_iota(jnp.int32, (8, 1), 0)
    head_fix = jnp.zeros((8, SSD_CONV_DIM), F32)
    for kk in range(1, taps + 1):
        head_fix = head_fix + cw_ref[taps - kk:taps - kk + 1, :] * jnp.where(row8 < kk, pltpu.roll(tail, kk, 0), 0.0)
    conv = jnp.concatenate([conv[:8] + head_fix, conv[8:]], axis=0)
    tail_ref[bb] = u[CHUNK - 8:]
    act = _silu(conv)
    xs = act[:, :SSD_W]
    bm = act[:, SSD_W:SSD_W + SSD_BC_W].astype(BF16)
    cm = act[:, SSD_W + SSD_BC_W:].astype(BF16)

    dt = jnp.where(valid, dt_ref[bb], 0.0)
    a = dt * (-jnp.exp(alog_ref[...]))
    acum = _dot_f32_left3(ltri_ref[...], a)
    acum_t = acum.T
    a_last = acum[CHUNK - 1:CHUNK, :]
    per_head = jnp.concatenate([dt, jnp.exp(acum), jnp.exp(a_last - acum)], axis=0)
    expanded = _dot_f32_right(per_head, eexp_ref[...])
    dt_e = expanded[:CHUNK]
    decay_in_e = expanded[CHUNK:2 * CHUNK]
    decay_out_e = expanded[2 * CHUNK:]
    xdt = xs * dt_e
    xdt_b = xdt.astype(BF16)
    xw_b = (xdt * decay_out_e).astype(BF16)

    row = lax.broadcasted_iota(jnp.int32, (CHUNK, CHUNK), 0)
    col = lax.broadcasted_iota(jnp.int32, (CHUNK, CHUNK), 1)
    causal = row >= col
    lane = lax.broadcasted_iota(jnp.int32, (CHUNK, LANES), 1)
    upper_half = lane >= HEAD_DIM

    ys = []
    for g in range(SSD_GROUPS):
        cg = cm[:, g * SSD_STATE:(g + 1) * SSD_STATE]
        bg = bm[:, g * SSD_STATE:(g + 1) * SSD_STATE]
        gcols = slice(g * group_w, (g + 1) * group_w)
        cb = _dot_nt(cg, bg)
        st = st_ref[bb, g]
        y_off = _dot(cg, st.astype(BF16)) * decay_in_e[:, gcols]
        pairs = []
        for pr in range(heads_per_group // 2):
            xp = xdt_b[:, g * group_w + pr * LANES:g * group_w + (pr + 1) * LANES]
            acc = None
            for hh in range(2):
                h = g * heads_per_group + pr * 2 + hh
                seg = acum[:, h:h + 1] - acum_t[h:h + 1, :]
                decay = jnp.where(causal, jnp.exp(jnp.where(causal, seg, 0.0)), 0.0)
                m = (cb * decay).astype(BF16)
                keep = upper_half if hh else jnp.logical_not(upper_half)
                t = _dot(m, jnp.where(keep, xp, jnp.zeros_like(xp)))
                acc = t if acc is None else acc + t
            pairs.append(acc)
        ys.append(jnp.concatenate(pairs, axis=1) + y_off)
        st_ref[bb, g] = st * decay_in_e[CHUNK - 1:CHUNK, gcols] + _dot_tn(bg, xw_b[:, gcols])

    y = jnp.concatenate(ys, axis=1) + xs * dexp_ref[...]
    y = y * zg_ref[bb]
    outs = []
    for g in range(SSD_GROUPS):
        yg = y[:, g * group_w:(g + 1) * group_w]
        gms = jnp.mean(yg * yg, axis=-1, keepdims=True)
        outs.append(yg * lax.rsqrt(gms + EPS) * nw_ref[:, g * group_w:(g + 1) * group_w])
    o_ref[bb] = jnp.concatenate(outs, axis=1).astype(BF16)


def _ssd(zg, xbc, dt, conv_w, conv_b, a_log, d_exp, norm_w, ltri, eexp, shift, bsz, length):
    nc = length // CHUNK
    per = _seqs_per_step(bsz)
    const = lambda b, c: (0, 0)
    rows = lambda b, c: (b, c, 0)
    group_w = SSD_W // SSD_GROUPS
    need = per * (2 * CHUNK * (SSD_W + SSD_CONV_DIM + DT_W) * 4 * 2 + 64 * CHUNK * SSD_CONV_DIM * 4)
    out = pl.pallas_call(
        _ssd_body,
        grid=(bsz // per, nc),
        in_specs=[
            pl.BlockSpec((per, CHUNK, SSD_W), rows),
            pl.BlockSpec((per, CHUNK, SSD_CONV_DIM), rows),
            pl.BlockSpec((per, CHUNK, DT_W), rows),
            pl.BlockSpec((SSD_CONV, SSD_CONV_DIM), const),
            pl.BlockSpec((1, SSD_CONV_DIM), const),
            pl.BlockSpec((1, DT_W), const),
            pl.BlockSpec((1, SSD_W), const),
            pl.BlockSpec((1, SSD_W), const),
            pl.BlockSpec((CHUNK, CHUNK), const),
            pl.BlockSpec((DT_W, SSD_W), const),
            pl.BlockSpec(((SSD_CONV - 1) * CHUNK, CHUNK), const),
        ],
        out_specs=pl.BlockSpec((per, CHUNK, SSD_W), rows),
        out_shape=jax.ShapeDtypeStruct((bsz, length, SSD_W), BF16),
        scratch_shapes=[pltpu.VMEM((per, 8, SSD_CONV_DIM), F32),
                        pltpu.VMEM((per, SSD_GROUPS, SSD_STATE, group_w), F32)],
        compiler_params=pltpu.CompilerParams(
            dimension_semantics=("parallel", "arbitrary"), vmem_limit_bytes=_vmem_limit(need)),
        name="ssd",
    )(zg.reshape(bsz, length, SSD_W), xbc.reshape(bsz, length, SSD_CONV_DIM), dt.reshape(bsz, length, DT_W),
      conv_w, conv_b, a_log, d_exp, norm_w, ltri, eexp, shift)
    return out.reshape(bsz * length, SSD_W)


def _hg_body(q_ref, lf_ref, k_ref, v_ref, g_ref, onw_ref, ltri2_ref, bd_ref,
             o_ref, st_ref, qs_ref, ks_ref, vs_ref, gs_ref, oacc_ref):
    c = pl.program_id(1)
    nsub = CHUNK // SUB
    per = q_ref.shape[0]

    @pl.when(c == 0)
    def _():
        st_ref[...] = jnp.zeros_like(st_ref)

    rowi = lax.broadcasted_iota(jnp.int32, (CHUNK, 1), 0)
    valid = (c * CHUNK + rowi) >= PAD
    srow = lax.broadcasted_iota(jnp.int32, (HG_W, HG_W), 0) // HG_DK
    scol = lax.broadcasted_iota(jnp.int32, (HG_W, HG_W), 1) // HG_DK
    erow = lax.broadcasted_iota(jnp.int32, (HG_HEADS * SUB, HG_W), 0) // SUB
    ecol = lax.broadcasted_iota(jnp.int32, (HG_HEADS * SUB, HG_W), 1) // HG_DK
    same_head = erow == ecol

    seqs = []
    for bb in range(per):
        log_f = jnp.where(valid, lf_ref[bb], 0.0)
        k = jnp.where(valid, k_ref[bb], 0.0)
        v = jnp.where(valid, v_ref[bb], 0.0)
        q = q_ref[bb]
        cums = _dot_f32_left3(ltri2_ref[...], log_f)
        gc = cums[:CHUNK]
        span = -cums[CHUNK:]
        g_last = gc[CHUNK - 1:CHUNK, :]

        st = st_ref[bb]
        oacc_ref[bb] = _dot_nt((q * jnp.exp(gc)).astype(BF16), st.astype(BF16))
        k_end = (k * jnp.exp(g_last - gc)).astype(BF16)
        v_b = v.astype(BF16)
        st_ref[bb] = st * jnp.exp(g_last) + jnp.where(srow == scol, _dot_tn(v_b, k_end), 0.0)

        seqs.append((q, k, v, gc, span))

    tri_row = lax.broadcasted_iota(jnp.int32, (SUB, HG_HEADS * SUB), 0)
    tri_col = lax.broadcasted_iota(jnp.int32, (SUB, HG_HEADS * SUB), 1) % SUB
    on_or_before = tri_col <= tri_row

    def key_blocks(bb, q, k, v, gc, own_rows):
        for jb in range(nsub if own_rows else nsub - 1):
            r0, r1 = jb * SUB, (jb + 1) * SUB
            lo = r0 if own_rows else r1
            g_end = gc[r1 - 1:r1, :]
            ke = k[r0:r1] * jnp.exp(g_end - gc[r0:r1])
            ke4 = jnp.where(same_head, jnp.concatenate([ke] * HG_HEADS, axis=0), 0.0).astype(BF16)
            v4 = jnp.where(same_head, jnp.concatenate([v[r0:r1]] * HG_HEADS, axis=0), 0.0).astype(BF16)
            qp = (q[lo:] * jnp.exp(gc[lo:] - g_end)).astype(BF16)
            scores = _dot_nt(qp, ke4)
            if own_rows:
                own = jnp.where(on_or_before, scores[:SUB], 0.0)
                scores = own if jb == nsub - 1 else jnp.concatenate([own, scores[SUB:]], axis=0)
            oacc_ref[bb, lo:, :] += _dot(scores.astype(BF16), v4)

    widest = seqs[0][4]
    for seq in seqs[1:]:
        widest = jnp.maximum(widest, seq[4])
    safe = jnp.max(widest) < HG_SAFE_SPAN

    @pl.when(safe)
    def _():
        for bb, (q, k, v, gc, span) in enumerate(seqs):
            key_blocks(bb, q, k, v, gc, True)

    @pl.when(jnp.logical_not(safe))
    def _():
        rin = lax.broadcasted_iota(jnp.int32, (SUB, 1), 0)
        for bb, (q, k, v, gc, span) in enumerate(seqs):
            key_blocks(bb, q, k, v, gc, False)
            qs_ref[...] = q
            ks_ref[...] = k
            vs_ref[...] = v
            gs_ref[...] = gc

            def diag_block(ib, carry, bb=bb):
                r0 = pl.multiple_of(ib * SUB, SUB)
                qi = qs_ref[pl.ds(r0, SUB), :]
                gi = gs_ref[pl.ds(r0, SUB), :]
                prods = []
                for j in range(SUB):
                    kj = ks_ref[pl.ds(r0 + j, 1), :]
                    gj = gs_ref[pl.ds(r0 + j, 1), :]
                    m = rin >= j
                    prods.append(jnp.where(m, qi * kj * jnp.exp(jnp.where(m, gi - gj, 0.0)), 0.0))
                head_sums = _dot_f32_right(jnp.concatenate(prods, axis=0), bd_ref[...])
                oi = jnp.zeros((SUB, HG_W), F32)
                for j in range(SUB):
                    oi = oi + head_sums[j * SUB:(j + 1) * SUB] * vs_ref[pl.ds(r0 + j, 1), :]
                oacc_ref[bb, pl.ds(r0, SUB), :] += oi
                return carry

            lax.fori_loop(0, nsub, diag_block, 0)

    for bb in range(per):
        o = oacc_ref[bb]
        hms = _dot_f32_right(o * o, bd_ref[...]) * (1.0 / HG_DK)
        o_ref[bb] = (o * lax.rsqrt(hms + EPS) * onw_ref[...] * g_ref[bb]).astype(BF16)


def _hgrn2(hq, hlf, hk, hv, hg, out_norm, ltri2, bd256, bsz, length):
    nc = length // CHUNK
    per = _seqs_per_step(bsz)
    const = lambda b, c: (0, 0)
    rows = lambda b, c: (b, c, 0)
    need = per * 64 * CHUNK * HG_W * 4
    seq3 = lambda t: t.reshape(bsz, length, HG_W)
    out = pl.pallas_call(
        _hg_body,
        grid=(bsz // per, nc),
        in_specs=[pl.BlockSpec((per, CHUNK, HG_W), rows)] * 5 + [
            pl.BlockSpec((1, HG_W), const),
            pl.BlockSpec((2 * CHUNK, CHUNK), const),
            pl.BlockSpec((HG_W, HG_W), const),
        ],
        out_specs=pl.BlockSpec((per, CHUNK, HG_W), rows),
        out_shape=jax.ShapeDtypeStruct((bsz, length, HG_W), BF16),
        scratch_shapes=[pltpu.VMEM((per, HG_W, HG_W), F32)] + [pltpu.VMEM((CHUNK, HG_W), F32)] * 4
        + [pltpu.VMEM((per, CHUNK, HG_W), F32)],
        compiler_params=pltpu.CompilerParams(
            dimension_semantics=("parallel", "arbitrary"), vmem_limit_bytes=_vmem_limit(need)),
        name="hgrn2",
    )(seq3(hq), seq3(hlf), seq3(hk), seq3(hv), seq3(hg), out_norm, ltri2, bd256)
    return out.reshape(bsz * length, HG_W)


def _out_mlp_body(n_views, *refs):
    wo_ref, nw_ref, wup_ref, wdn_ref, out_ref = refs[4 * n_views:]

    def rows(k):
        return jnp.concatenate([r[...] for r in refs[k * n_views:(k + 1) * n_views]], axis=0)

    h1 = (rows(3)
          + _dot(rows(0), wo_ref[0:SB_W, :])
          + _dot(rows(1), wo_ref[SB_W:SB_W + SSD_W, :])
          + _dot(rows(2), wo_ref[SB_W + SSD_W:, :]))
    ms = jnp.mean(h1 * h1, axis=-1, keepdims=True)
    hn = (h1 * lax.rsqrt(ms + EPS) * nw_ref[...]).astype(BF16)
    mlp = jnp.zeros_like(h1)
    for c in range(wup_ref.shape[1] // FF_BLOCK):
        u = _dot(hn, wup_ref[:, c * FF_BLOCK:(c + 1) * FF_BLOCK])
        act = jnp.square(jnp.maximum(u, 0.0)).astype(BF16)
        mlp = mlp + _dot(act, wdn_ref[c * FF_BLOCK:(c + 1) * FF_BLOCK, :])
    out_ref[...] = h1 + mlp


def _out_mlp(o_sb, o_ssd, o_hg, h, w_out, norm_w, w_up, w_down, drop_lead=None):
    rows, d = h.shape
    d_ff = w_up.shape[1]
    const = lambda i: (0, 0)
    operands = (o_sb, o_ssd, o_hg, h)
    if drop_lead is None:
        tm = _row_tile(rows, 512)
        n_views, grid = 1, rows // tm
        row_args = list(operands)
        row_specs = [pl.BlockSpec((tm, t.shape[1]), lambda i: (i, 0)) for t in operands]
        out_spec = pl.BlockSpec((tm, d), lambda i: (i, 0))
        out_shape = jax.ShapeDtypeStruct((rows, d), F32)
    else:
        bsz, length = drop_lead
        chunks_seq = length // CHUNK
        tm = _row_tile(length, 512)
        n_views = tm // CHUNK
        tiles_per_seq = pl.cdiv(length - CHUNK, tm)

        def view(width, m):
            return pl.BlockSpec((None, CHUNK, width), lambda i: (
                (i // tiles_per_seq) * chunks_seq
                + jnp.minimum(1 + (i % tiles_per_seq) * n_views + m, chunks_seq - 1), 0, 0))

        grid = bsz * tiles_per_seq
        row_args = [t.reshape(rows // CHUNK, CHUNK, t.shape[1]) for t in operands for _ in range(n_views)]
        row_specs = [view(t.shape[1], m) for t in operands for m in range(n_views)]
        out_spec = pl.BlockSpec((None, tm, d), lambda i: (i // tiles_per_seq, i % tiles_per_seq, 0))
        out_shape = jax.ShapeDtypeStruct((bsz, length - CHUNK, d), F32)
    weights = (w_out.shape[0] * d + 2 * d * d_ff) * 2
    need = 2 * weights + 2 * tm * (2 * d * 4 + (SB_W + SSD_W + HG_W) * 2) + 6 * tm * FF_BLOCK * 4
    return pl.pallas_call(
        functools.partial(_out_mlp_body, n_views),
        grid=(grid,),
        in_specs=row_specs + [
            pl.BlockSpec(w_out.shape, const),
            pl.BlockSpec((1, d), const),
            pl.BlockSpec(w_up.shape, const),
            pl.BlockSpec(w_down.shape, const),
        ],
        out_specs=out_spec,
        out_shape=out_shape,
        compiler_params=pltpu.CompilerParams(
            dimension_semantics=("parallel",), vmem_limit_bytes=_vmem_limit(need)),
        name="out_mlp",
    )(*row_args, w_out, norm_w, w_up, w_down)


def _block_diag_ones(n, block):
    idx = np.arange(n) // block
    return jnp.asarray(idx[:, None] == idx[None, :], BF16)


def _constants():
    t = np.arange(CHUNK)
    ltri = jnp.asarray(t[None, :] <= t[:, None], BF16)
    later = (t[:, None] > t[None, :])
    usum = jnp.asarray(np.concatenate([later, np.ones((CHUNK, CHUNK), bool)], axis=1), BF16)
    eexp = np.zeros((DT_W, SSD_W), bool)
    for h in range(SSD_HEADS):
        eexp[h, h * HEAD_DIM:(h + 1) * HEAD_DIM] = True
    in_sub = (t[None, :] <= t[:, None]) & (t[None, :] // SUB == t[:, None] // SUB)
    ltri2 = jnp.asarray(np.concatenate([t[None, :] <= t[:, None], in_sub], axis=0), BF16)
    shift = jnp.asarray(np.concatenate([t[None, :] == t[:, None] - k for k in range(1, SSD_CONV)], axis=0), BF16)
    return ltri, ltri2, usum, jnp.asarray(eexp, BF16), shift


def _pack_w_in(w):
    d = w.shape[0]
    dt_lo = 3 * SB_W + 2 * SSD_W + 2 * SSD_BC_W
    dt_hi = dt_lo + SSD_HEADS
    return jnp.concatenate(
        [w[:, :dt_lo], w[:, dt_lo:dt_hi], jnp.zeros((d, DT_W - SSD_HEADS), w.dtype), w[:, dt_hi:]],
        axis=1).astype(BF16)


def _pad_lanes(v, width):
    return jnp.pad(v.astype(F32), (0, width - v.shape[0]))[None, :]


def kernel(x, meta_tokens, hg_lb_logits, norm_mix_w, w_in, sb_q_norm, sb_k_norm, sb_out_norm, ssd_conv_w,
           ssd_conv_b, ssd_dt_bias, ssd_A_log, ssd_D, ssd_norm_w, hg_out_norm, w_out, norm_mlp_w, w_up, w_down):
    bsz, seq, d = x.shape
    depth = w_in.shape[0]
    length = seq + CHUNK
    lead = jnp.concatenate([jnp.zeros((PAD, d), x.dtype), meta_tokens.astype(x.dtype)], axis=0)
    h = x

    probs = jax.nn.softmax(hg_lb_logits.astype(F32), axis=0)
    lbs = jnp.concatenate([jnp.zeros_like(probs[0:1]), jnp.cumsum(probs, axis=0)[:-1]], axis=0)

    ltri, ltri2, usum, eexp, shift = _constants()
    bd256 = _block_diag_ones(SB_W, HEAD_DIM)
    bd128 = _block_diag_ones(LANES, HEAD_DIM)

    for l in range(depth):
        proj = _in_proj(
            h, norm_mix_w[l][None, :], _pack_w_in(w_in[l]),
            jnp.tile(sb_q_norm[l], SB_HEADS)[None, :], jnp.tile(sb_k_norm[l], SB_HEADS)[None, :], bd256,
            _pad_lanes(ssd_dt_bias[l], DT_W), lbs[l][None, :], length, lead if l == 0 else None)
        if l == 0:
            h = proj[-1]
        q, k, v, zg, xbc, dt, hq, hlf, hk, hv, hg = proj[:11]
        o_sb = _sb_attn(q, k, v, sb_out_norm[l].reshape(1, SB_W), usum, bd128, bsz, length)
        o_ssd = _ssd(zg, xbc, dt, ssd_conv_w[l], ssd_conv_b[l][None, :],
                     _pad_lanes(ssd_A_log[l], DT_W), jnp.repeat(ssd_D[l].astype(F32), HEAD_DIM)[None, :],
                     ssd_norm_w[l].reshape(1, SSD_W), ltri, eexp, shift, bsz, length)
        o_hg = _hgrn2(hq, hlf, hk, hv, hg, hg_out_norm[l].reshape(1, HG_W), ltri2, bd256, bsz, length)
        h = _out_mlp(o_sb, o_ssd, o_hg, h, w_out[l].astype(BF16), norm_mlp_w[l][None, :],
                     w_up[l].astype(BF16), w_down[l].astype(BF16), (bsz, length) if l == depth - 1 else None)
    return h
```

```python
import functools

import numpy as np
import jax
import jax.numpy as jnp
from jax import lax
from jax.experimental import pallas as pl
from jax.experimental.pallas import tpu as pltpu

F32 = jnp.float32
BF16 = jnp.bfloat16

N_META = 16
CHUNK = 128
PAD = CHUNK - N_META
HEAD_DIM = 64
SB_HEADS = 4
SB_W = SB_HEADS * HEAD_DIM
SSD_HEADS = 8
SSD_W = SSD_HEADS * HEAD_DIM
SSD_GROUPS = 2
SSD_STATE = 128
SSD_CONV = 4
SSD_BC_W = SSD_GROUPS * SSD_STATE
SSD_CONV_DIM = SSD_W + 2 * SSD_BC_W
HG_HEADS = 4
HG_DK = 64
HG_W = HG_HEADS * HG_DK
EPS = 1e-6
TINY = 1e-30
LOG2E = 1.4426950408889634
HG_SAFE_SPAN = 60.0
SUB = 16
SB_BLOCK = 3 * CHUNK
SB_DEAD_MASS = 150.0
LANES = 128
DT_W = LANES
FF_BLOCK = 1024
VMEM_CAP_V7X = 64 * 1024 * 1024

_C_Q, _C_K, _C_V = 0, SB_W, 2 * SB_W
_C_Z = 3 * SB_W
_C_XBC = _C_Z + SSD_W
_C_DT = _C_XBC + SSD_CONV_DIM
_C_HQ = _C_DT + DT_W
_C_HF = _C_HQ + HG_W
_C_HI = _C_HF + HG_W
_C_HG = _C_HI + HG_W
D_IN_PACKED = _C_HG + HG_W


def _vmem_limit(need_bytes):
    return int(min(max(need_bytes, 32 * 1024 * 1024), VMEM_CAP_V7X - 6 * 1024 * 1024))


def _seqs_per_step(bsz):
    return next(n for n in (4, 2, 1) if bsz % n == 0)


def _row_tile(rows, target):
    t = min(target, rows)
    while rows % t:
        t -= CHUNK
    return t


def _dot(a, b):
    return jnp.dot(a, b, preferred_element_type=F32)


def _dot_nt(a, b):
    return lax.dot_general(a, b, (((1,), (1,)), ((), ())), preferred_element_type=F32)


def _dot_tn(a, b):
    return lax.dot_general(a, b, (((0,), (0,)), ((), ())), preferred_element_type=F32)


def _split2(x):
    hi = x.astype(BF16)
    lo = (x - hi.astype(F32)).astype(BF16)
    return hi, lo


def _split3(x):
    hi = x.astype(BF16)
    r = x - hi.astype(F32)
    mid = r.astype(BF16)
    lo = (r - mid.astype(F32)).astype(BF16)
    return hi, mid, lo


def _dot_f32_right(x, m):
    hi, lo = _split2(x)
    return _dot(jnp.concatenate([hi, lo], axis=1), jnp.concatenate([m, m], axis=0))


def _dot_f32_left3(m, x):
    hi, mid, lo = _split3(x)
    return _dot(jnp.concatenate([m, m, m], axis=1), jnp.concatenate([hi, mid, lo], axis=0))


def _softplus(x):
    return jnp.maximum(x, 0.0) + jnp.log(1.0 + jnp.exp(-jnp.abs(x)))


def _sigmoid(x):
    return 1.0 / (1.0 + jnp.exp(-x))


def _silu(x):
    return x * _sigmoid(x)


def _in_proj_body(tiles_per_seq, from_input, *refs):
    if from_input:
        n_views = len(refs) - 20
        lead_ref, view_refs, refs = refs[0], refs[1:1 + n_views], refs[1 + n_views:]
        h_out_ref = refs[-1]
    else:
        h_ref, refs = refs[0], refs[1:]
    (nw_ref, w_ref, qn_ref, kn_ref, bd_ref, dtb_ref, lb_ref,
     q_ref, k_ref, v_ref, zg_ref, xbc_ref, dt_ref, hq_ref, hlf_ref, hk_ref, hv_ref, hg_ref) = refs[:18]
    tile = pl.program_id(0) % tiles_per_seq
    if from_input:
        first = jnp.where(tile == 0, lead_ref[...], view_refs[0][...])
        x = jnp.concatenate([first] + [r[...] for r in view_refs[1:]], axis=0)
        h_out_ref[...] = x
    else:
        x = h_ref[...]
    tm = x.shape[0]
    rowi = lax.broadcasted_iota(jnp.int32, (tm, 1), 0)
    valid = tile * tm + rowi >= PAD
    ms = jnp.mean(x * x, axis=-1, keepdims=True)
    hn = (x * lax.rsqrt(ms + EPS) * nw_ref[...]).astype(BF16)

    def seg(lo, width):
        return _dot(hn, w_ref[:, lo:lo + width])

    def head_norm(t, w):
        hms = _dot_f32_right(t * t, bd_ref[...]) * (1.0 / HEAD_DIM)
        return t * lax.rsqrt(hms + EPS) * w

    q = head_norm(seg(_C_Q, SB_W), qn_ref[...])
    q_ref[...] = (q * (HEAD_DIM ** -0.5 * LOG2E)).astype(BF16)
    k_ref[...] = jnp.where(valid, head_norm(seg(_C_K, SB_W), kn_ref[...]), 0.0).astype(BF16)
    v_ref[...] = jnp.where(valid, seg(_C_V, SB_W), 0.0).astype(BF16)
    zg_ref[...] = _silu(seg(_C_Z, SSD_W))
    xbc_ref[...] = seg(_C_XBC, SSD_CONV_DIM)
    dt_ref[...] = _softplus(seg(_C_DT, DT_W) + dtb_ref[...])
    hq_ref[...] = _silu(seg(_C_HQ, HG_W))
    lb = lb_ref[...]
    fl = seg(_C_HF, HG_W)
    gate = _sigmoid(fl)
    hlf_ref[...] = jnp.log(jnp.maximum(lb + (1.0 - lb) * gate, TINY))
    hk_ref[...] = (1.0 - lb) * (1.0 - gate)
    hv_ref[...] = seg(_C_HI, HG_W)
    hg_ref[...] = _silu(seg(_C_HG, HG_W))


def _in_proj(h, norm_w, w_packed, qn, kn, bd256, dt_bias, lb, length, lead=None):
    from_input = lead is not None
    d = h.shape[-1]
    rows = h.shape[0] * length if from_input else h.shape[0]
    tm = _row_tile(length, 512)
    tiles_per_seq = length // tm
    widths = (SB_W, SB_W, SB_W, SSD_W, SSD_CONV_DIM, DT_W, HG_W, HG_W, HG_W, HG_W, HG_W)
    dtypes = (BF16, BF16, BF16, F32, F32, F32, F32, F32, F32, F32, F32)
    const = lambda i: (0, 0)
    if from_input:
        chunks_in = (length - CHUNK) // CHUNK
        per_tile = tm // CHUNK

        def view(m):
            return pl.BlockSpec((None, CHUNK, d), lambda i: (
                (i // tiles_per_seq) * chunks_in + jnp.maximum((i % tiles_per_seq) * per_tile + m - 1, 0), 0, 0))

        chunks = h.reshape(h.shape[0] * chunks_in, CHUNK, d)
        row_args = [lead] + [chunks] * per_tile
        row_specs = [pl.BlockSpec((CHUNK, d), const)] + [view(m) for m in range(per_tile)]
        widths, dtypes = widths + (d,), dtypes + (F32,)
    else:
        row_args = [h]
        row_specs = [pl.BlockSpec((tm, d), lambda i: (i, 0))]
    out_bytes = sum(w * jnp.dtype(t).itemsize for w, t in zip(widths, dtypes)) * tm
    need = 2 * (tm * d * 4 + d * D_IN_PACKED * 2 + out_bytes) + tm * SSD_CONV_DIM * 4 * 4
    return pl.pallas_call(
        functools.partial(_in_proj_body, tiles_per_seq, from_input),
        grid=(rows // tm,),
        in_specs=row_specs + [
            pl.BlockSpec((1, d), const),
            pl.BlockSpec((d, D_IN_PACKED), const),
            pl.BlockSpec((1, SB_W), const),
            pl.BlockSpec((1, SB_W), const),
            pl.BlockSpec((SB_W, SB_W), const),
            pl.BlockSpec((1, DT_W), const),
            pl.BlockSpec((1, HG_W), const),
        ],
        out_specs=[pl.BlockSpec((tm, w), lambda i: (i, 0)) for w in widths],
        out_shape=[jax.ShapeDtypeStruct((rows, w), t) for w, t in zip(widths, dtypes)],
        compiler_params=pltpu.CompilerParams(
            dimension_semantics=("parallel",), vmem_limit_bytes=_vmem_limit(need)),
        name="in_proj",
    )(*row_args, norm_w, w_packed, qn, kn, bd256, dt_bias, lb)


def _sb_body(q_ref, k_ref, v_ref, onw_ref, usum_ref, bd_ref, o_ref, acc_ref, carry_ref):
    ib = pl.program_id(2)
    nsub = SB_BLOCK // CHUNK
    lane = lax.broadcasted_iota(jnp.int32, (SB_BLOCK, LANES), 1)
    q = q_ref[...]
    zero = jnp.zeros_like(q)
    q2 = jnp.concatenate([jnp.where(lane < HEAD_DIM, q, zero), jnp.where(lane >= HEAD_DIM, q, zero)], axis=0)
    acc_ref[...] = jnp.zeros_like(acc_ref)
    carry_ref[...] = jnp.zeros_like(carry_ref)

    def block_rows(jb):
        return pl.ds(pl.multiple_of(jb * SB_BLOCK, SB_BLOCK), SB_BLOCK)

    def logits(jb):
        return _dot_nt(q2, k_ref[block_rows(jb), :])

    def accumulate(w, jb):
        acc_ref[...] += _dot(w, v_ref[block_rows(jb), :])

    def drop_mass(z):
        return jnp.maximum(z, 0.0) + jnp.log2(1.0 + jnp.exp2(-jnp.abs(z)))

    def weights(z, mask):
        sp = drop_mass(z)
        drop = (sp if mask is None else jnp.where(mask, sp, 0.0)).astype(BF16)
        carry = carry_ref[...]
        ws = [None] * nsub
        for m in reversed(range(nsub)):
            cols = slice(m * CHUNK, (m + 1) * CHUNK)
            sums = _dot(drop[:, cols], usum_ref[...])
            log_w = (z[:, cols] - sp[:, cols]) - sums[:, :CHUNK] - carry
            carry = carry + sums[:, CHUNK:]
            if mask is None:
                w = jnp.exp2(log_w)
            else:
                w = jnp.where(mask[:, cols], jnp.exp2(jnp.where(mask[:, cols], log_w, 0.0)), 0.0)
            ws[m] = w.astype(BF16)
        carry_ref[...] = carry
        return jnp.concatenate(ws, axis=1)

    row = lax.broadcasted_iota(jnp.int32, (SB_BLOCK, SB_BLOCK), 0)
    col = lax.broadcasted_iota(jnp.int32, (SB_BLOCK, SB_BLOCK), 1)
    causal = col < row
    accumulate(weights(logits(ib), jnp.concatenate([causal, causal], axis=0)), ib)

    def live():
        return (jnp.min(carry_ref[...]) < SB_DEAD_MASS).astype(jnp.int32)

    def cond(state):
        jb, alive = state
        return jnp.logical_and(jb >= 0, alive > 0)

    def body(state):
        jb, _ = state
        accumulate(weights(logits(jb), None), jb)
        return jb - 1, live()

    lax.while_loop(cond, body, (ib - 1, live()))

    o = jnp.where(lane < HEAD_DIM, acc_ref[0:SB_BLOCK, :], acc_ref[SB_BLOCK:, :])
    hms = _dot_f32_right(o * o, bd_ref[...]) * (1.0 / HEAD_DIM)
    o_ref[...] = (o * lax.rsqrt(hms + EPS) * onw_ref[...]).astype(BF16)


def _sb_attn(q, k, v, out_norm, usum, bd128, bsz, length):
    assert length % SB_BLOCK == 0
    nb = length // SB_BLOCK
    const = lambda b, p, i: (0, 0)
    need = 2 * (2 * length * LANES * 2) + 24 * 2 * SB_BLOCK * SB_BLOCK * 4
    return pl.pallas_call(
        _sb_body,
        grid=(bsz, SB_W // LANES, nb),
        in_specs=[
            pl.BlockSpec((SB_BLOCK, LANES), lambda b, p, i: (b * nb + i, p)),
            pl.BlockSpec((length, LANES), lambda b, p, i: (b, p)),
            pl.BlockSpec((length, LANES), lambda b, p, i: (b, p)),
            pl.BlockSpec((1, LANES), lambda b, p, i: (0, p)),
            pl.BlockSpec((CHUNK, 2 * CHUNK), const),
            pl.BlockSpec((LANES, LANES), const),
        ],
        out_specs=pl.BlockSpec((SB_BLOCK, LANES), lambda b, p, i: (b * nb + i, p)),
        out_shape=jax.ShapeDtypeStruct((bsz * length, SB_W), BF16),
        scratch_shapes=[pltpu.VMEM((2 * SB_BLOCK, LANES), F32), pltpu.VMEM((2 * SB_BLOCK, CHUNK), F32)],
        compiler_params=pltpu.CompilerParams(
            dimension_semantics=("parallel", "parallel", "arbitrary"), vmem_limit_bytes=_vmem_limit(need)),
        name="sb_attn",
    )(q, k, v, out_norm, usum, bd128)


def _ssd_body(zg_ref, xbc_ref, dt_ref, cw_ref, cb_ref, alog_ref, dexp_ref, nw_ref, ltri_ref, eexp_ref, shift_ref,
              o_ref, tail_ref, st_ref):
    c = pl.program_id(1)

    @pl.when(c == 0)
    def _():
        st_ref[...] = jnp.zeros_like(st_ref)
        tail_ref[...] = jnp.zeros_like(tail_ref)

    for bb in range(zg_ref.shape[0]):
        _ssd_chunk(bb, c, zg_ref, xbc_ref, dt_ref, cw_ref, cb_ref, alog_ref, dexp_ref, nw_ref, ltri_ref, eexp_ref,
                   shift_ref, o_ref, tail_ref, st_ref)


def _ssd_chunk(bb, c, zg_ref, xbc_ref, dt_ref, cw_ref, cb_ref, alog_ref, dexp_ref, nw_ref, ltri_ref, eexp_ref,
               shift_ref, o_ref, tail_ref, st_ref):
    heads_per_group = SSD_HEADS // SSD_GROUPS
    group_w = SSD_W // SSD_GROUPS
    taps = SSD_CONV - 1

    rowi = lax.broadcasted_iota(jnp.int32, (CHUNK, 1), 0)
    valid = (c * CHUNK + rowi) >= PAD
    u = jnp.where(valid, xbc_ref[bb], 0.0)
    shifted = _dot(shift_ref[...], u.astype(BF16))
    conv = cb_ref[...] + cw_ref[taps:taps + 1, :] * u
    for kk in range(1, taps + 1):
        conv = conv + cw_ref[taps - kk:taps - kk + 1, :] * shifted[(kk - 1) * CHUNK:kk * CHUNK]
    tail = tail_ref[bb]
    row8 = lax.broadcasted_iota(jnp.int32, (8, 1), 0)
    head_fix = jnp.zeros((8, SSD_CONV_DIM), F32)
    for kk in range(1, taps + 1):
        head_fix = head_fix + cw_ref[taps - kk:taps - kk + 1, :] * jnp.where(row8 < kk, pltpu.roll(tail, kk, 0), 0.0)
    conv = jnp.concatenate([conv[:8] + head_fix, conv[8:]], axis=0)
    tail_ref[bb] = u[CHUNK - 8:]
    act = _silu(conv)
    xs = act[:, :SSD_W]
    bm = act[:, SSD_W:SSD_W + SSD_BC_W].astype(BF16)
    cm = act[:, SSD_W + SSD_BC_W:].astype(BF16)

    dt = jnp.where(valid, dt_ref[bb], 0.0)
    a = dt * (-jnp.exp(alog_ref[...]))
    acum = _dot_f32_left3(ltri_ref[...], a)
    acum_t = acum.T
    a_last = acum[CHUNK - 1:CHUNK, :]
    per_head = jnp.concatenate([dt, jnp.exp(acum), jnp.exp(a_last - acum)], axis=0)
    expanded = _dot_f32_right(per_head, eexp_ref[...])
    dt_e = expanded[:CHUNK]
    decay_in_e = expanded[CHUNK:2 * CHUNK]
    decay_out_e = expanded[2 * CHUNK:]
    xdt = xs * dt_e
    xdt_b = xdt.astype(BF16)
    xw_b = (xdt * decay_out_e).astype(BF16)

    row = lax.broadcasted_iota(jnp.int32, (CHUNK, CHUNK), 0)
    col = lax.broadcasted_iota(jnp.int32, (CHUNK, CHUNK), 1)
    causal = row >= col
    lane = lax.broadcasted_iota(jnp.int32, (CHUNK, LANES), 1)
    upper_half = lane >= HEAD_DIM

    ys = []
    for g in range(SSD_GROUPS):
        cg = cm[:, g * SSD_STATE:(g + 1) * SSD_STATE]
        bg = bm[:, g * SSD_STATE:(g + 1) * SSD_STATE]
        gcols = slice(g * group_w, (g + 1) * group_w)
        cb = _dot_nt(cg, bg)
        st = st_ref[bb, g]
        y_off = _dot(cg, st.astype(BF16)) * decay_in_e[:, gcols]
        pairs = []
        for pr in range(heads_per_group // 2):
            xp = xdt_b[:, g * group_w + pr * LANES:g * group_w + (pr + 1) * LANES]
            acc = None
            for hh in range(2):
                h = g * heads_per_group + pr * 2 + hh
                seg = acum[:, h:h + 1] - acum_t[h:h + 1, :]
                decay = jnp.where(causal, jnp.exp(jnp.where(causal, seg, 0.0)), 0.0)
                m = (cb * decay).astype(BF16)
                keep = upper_half if hh else jnp.logical_not(upper_half)
                t = _dot(m, jnp.where(keep, xp, jnp.zeros_like(xp)))
                acc = t if acc is None else acc + t
            pairs.append(acc)
        ys.append(jnp.concatenate(pairs, axis=1) + y_off)
        st_ref[bb, g] = st * decay_in_e[CHUNK - 1:CHUNK, gcols] + _dot_tn(bg, xw_b[:, gcols])

    y = jnp.concatenate(ys, axis=1) + xs * dexp_ref[...]
    y = y * zg_ref[bb]
    outs = []
    for g in range(SSD_GROUPS):
        yg = y[:, g * group_w:(g + 1) * group_w]
        gms = jnp.mean(yg * yg, axis=-1, keepdims=True)
        outs.append(yg * lax.rsqrt(gms + EPS) * nw_ref[:, g * group_w:(g + 1) * group_w])
    o_ref[bb] = jnp.concatenate(outs, axis=1).astype(BF16)


def _ssd(zg, xbc, dt, conv_w, conv_b, a_log, d_exp, norm_w, ltri, eexp, shift, bsz, length):
    nc = length // CHUNK
    per = _seqs_per_step(bsz)
    const = lambda b, c: (0, 0)
    rows = lambda b, c: (b, c, 0)
    group_w = SSD_W // SSD_GROUPS
    need = per * (2 * CHUNK * (SSD_W + SSD_CONV_DIM + DT_W) * 4 * 2 + 64 * CHUNK * SSD_CONV_DIM * 4)
    out = pl.pallas_call(
        _ssd_body,
        grid=(bsz // per, nc),
        in_specs=[
            pl.BlockSpec((per, CHUNK, SSD_W), rows),
            pl.BlockSpec((per, CHUNK, SSD_CONV_DIM), rows),
            pl.BlockSpec((per, CHUNK, DT_W), rows),
            pl.BlockSpec((SSD_CONV, SSD_CONV_DIM), const),
            pl.BlockSpec((1, SSD_CONV_DIM), const),
            pl.BlockSpec((1, DT_W), const),
            pl.BlockSpec((1, SSD_W), const),
            pl.BlockSpec((1, SSD_W), const),
            pl.BlockSpec((CHUNK, CHUNK), const),
            pl.BlockSpec((DT_W, SSD_W), const),
            pl.BlockSpec(((SSD_CONV - 1) * CHUNK, CHUNK), const),
        ],
        out_specs=pl.BlockSpec((per, CHUNK, SSD_W), rows),
        out_shape=jax.ShapeDtypeStruct((bsz, length, SSD_W), BF16),
        scratch_shapes=[pltpu.VMEM((per, 8, SSD_CONV_DIM), F32),
                        pltpu.VMEM((per, SSD_GROUPS, SSD_STATE, group_w), F32)],
        compiler_params=pltpu.CompilerParams(
            dimension_semantics=("parallel", "arbitrary"), vmem_limit_bytes=_vmem_limit(need)),
        name="ssd",
    )(zg.reshape(bsz, length, SSD_W), xbc.reshape(bsz, length, SSD_CONV_DIM), dt.reshape(bsz, length, DT_W),
      conv_w, conv_b, a_log, d_exp, norm_w, ltri, eexp, shift)
    return out.reshape(bsz * length, SSD_W)


def _hg_body(q_ref, lf_ref, k_ref, v_ref, g_ref, onw_ref, ltri2_ref, bd_ref,
             o_ref, st_ref, qs_ref, ks_ref, vs_ref, gs_ref, oacc_ref):
    c = pl.program_id(1)
    nsub = CHUNK // SUB
    per = q_ref.shape[0]

    @pl.when(c == 0)
    def _():
        st_ref[...] = jnp.zeros_like(st_ref)

    rowi = lax.broadcasted_iota(jnp.int32, (CHUNK, 1), 0)
    valid = (c * CHUNK + rowi) >= PAD
    srow = lax.broadcasted_iota(jnp.int32, (HG_W, HG_W), 0) // HG_DK
    scol = lax.broadcasted_iota(jnp.int32, (HG_W, HG_W), 1) // HG_DK
    erow = lax.broadcasted_iota(jnp.int32, (HG_HEADS * SUB, HG_W), 0) // SUB
    ecol = lax.broadcasted_iota(jnp.int32, (HG_HEADS * SUB, HG_W), 1) // HG_DK
    same_head = erow == ecol

    seqs = []
    for bb in range(per):
        log_f = jnp.where(valid, lf_ref[bb], 0.0)
        k = jnp.where(valid, k_ref[bb], 0.0)
        v = jnp.where(valid, v_ref[bb], 0.0)
        q = q_ref[bb]
        cums = _dot_f32_left3(ltri2_ref[...], log_f)
        gc = cums[:CHUNK]
        span = -cums[CHUNK:]
        g_last = gc[CHUNK - 1:CHUNK, :]

        st = st_ref[bb]
        oacc_ref[bb] = _dot_nt((q * jnp.exp(gc)).astype(BF16), st.astype(BF16))
        k_end = (k * jnp.exp(g_last - gc)).astype(BF16)
        v_b = v.astype(BF16)
        st_ref[bb] = st * jnp.exp(g_last) + jnp.where(srow == scol, _dot_tn(v_b, k_end), 0.0)

        seqs.append((q, k, v, gc, span))

    tri_row = lax.broadcasted_iota(jnp.int32, (SUB, HG_HEADS * SUB), 0)
    tri_col = lax.broadcasted_iota(jnp.int32, (SUB, HG_HEADS * SUB), 1) % SUB
    on_or_before = tri_col <= tri_row

    def key_blocks(bb, q, k, v, gc, own_rows):
        for jb in range(nsub if own_rows else nsub - 1):
            r0, r1 = jb * SUB, (jb + 1) * SUB
            lo = r0 if own_rows else r1
            g_end = gc[r1 - 1:r1, :]
            ke = k[r0:r1] * jnp.exp(g_end - gc[r0:r1])
            ke4 = jnp.where(same_head, jnp.concatenate([ke] * HG_HEADS, axis=0), 0.0).astype(BF16)
            v4 = jnp.where(same_head, jnp.concatenate([v[r0:r1]] * HG_HEADS, axis=0), 0.0).astype(BF16)
            qp = (q[lo:] * jnp.exp(gc[lo:] - g_end)).astype(BF16)
            scores = _dot_nt(qp, ke4)
            if own_rows:
                own = jnp.where(on_or_before, scores[:SUB], 0.0)
                scores = own if jb == nsub - 1 else jnp.concatenate([own, scores[SUB:]], axis=0)
            oacc_ref[bb, lo:, :] += _dot(scores.astype(BF16), v4)

    widest = seqs[0][4]
    for seq in seqs[1:]:
        widest = jnp.maximum(widest, seq[4])
    safe = jnp.max(widest) < HG_SAFE_SPAN

    @pl.when(safe)
    def _():
        for bb, (q, k, v, gc, span) in enumerate(seqs):
            key_blocks(bb, q, k, v, gc, True)

    @pl.when(jnp.logical_not(safe))
    def _():
        rin = lax.broadcasted_iota(jnp.int32, (SUB, 1), 0)
        for bb, (q, k, v, gc, span) in enumerate(seqs):
            key_blocks(bb, q, k, v, gc, False)
            qs_ref[...] = q
            ks_ref[...] = k
            vs_ref[...] = v
            gs_ref[...] = gc

            def diag_block(ib, carry, bb=bb):
                r0 = pl.multiple_of(ib * SUB, SUB)
                qi = qs_ref[pl.ds(r0, SUB), :]
                gi = gs_ref[pl.ds(r0, SUB), :]
                prods = []
                for j in range(SUB):
                    kj = ks_ref[pl.ds(r0 + j, 1), :]
                    gj = gs_ref[pl.ds(r0 + j, 1), :]
                    m = rin >= j
                    prods.append(jnp.where(m, qi * kj * jnp.exp(jnp.where(m, gi - gj, 0.0)), 0.0))
                head_sums = _dot_f32_right(jnp.concatenate(prods, axis=0), bd_ref[...])
                oi = jnp.zeros((SUB, HG_W), F32)
                for j in range(SUB):
                    oi = oi + head_sums[j * SUB:(j + 1) * SUB] * vs_ref[pl.ds(r0 + j, 1), :]
                oacc_ref[bb, pl.ds(r0, SUB), :] += oi
                return carry

            lax.fori_loop(0, nsub, diag_block, 0)

    for bb in range(per):
        o = oacc_ref[bb]
        hms = _dot_f32_right(o * o, bd_ref[...]) * (1.0 / HG_DK)
        o_ref[bb] = (o * lax.rsqrt(hms + EPS) * onw_ref[...] * g_ref[bb]).astype(BF16)


def _hgrn2(hq, hlf, hk, hv, hg, out_norm, ltri2, bd256, bsz, length):
    nc = length // CHUNK
    per = _seqs_per_step(bsz)
    const = lambda b, c: (0, 0)
    rows = lambda b, c: (b, c, 0)
    need = per * 64 * CHUNK * HG_W * 4
    seq3 = lambda t: t.reshape(bsz, length, HG_W)
    out = pl.pallas_call(
        _hg_body,
        grid=(bsz // per, nc),
        in_specs=[pl.BlockSpec((per, CHUNK, HG_W), rows)] * 5 + [
            pl.BlockSpec((1, HG_W), const),
            pl.BlockSpec((2 * CHUNK, CHUNK), const),
            pl.BlockSpec((HG_W, HG_W), const),
        ],
        out_specs=pl.BlockSpec((per, CHUNK, HG_W), rows),
        out_shape=jax.ShapeDtypeStruct((bsz, length, HG_W), BF16),
        scratch_shapes=[pltpu.VMEM((per, HG_W, HG_W), F32)] + [pltpu.VMEM((CHUNK, HG_W), F32)] * 4
        + [pltpu.VMEM((per, CHUNK, HG_W), F32)],
        compiler_params=pltpu.CompilerParams(
            dimension_semantics=("parallel", "arbitrary"), vmem_limit_bytes=_vmem_limit(need)),
        name="hgrn2",
    )(seq3(hq), seq3(hlf), seq3(hk), seq3(hv), seq3(hg), out_norm, ltri2, bd256)
    return out.reshape(bsz * length, HG_W)


def _out_mlp_body(n_views, *refs):
    wo_ref, nw_ref, wup_ref, wdn_ref, out_ref = refs[4 * n_views:]

    def rows(k):
        return jnp.concatenate([r[...] for r in refs[k * n_views:(k + 1) * n_views]], axis=0)

    h1 = (rows(3)
          + _dot(rows(0), wo_ref[0:SB_W, :])
          + _dot(rows(1), wo_ref[SB_W:SB_W + SSD_W, :])
          + _dot(rows(2), wo_ref[SB_W + SSD_W:, :]))
    ms = jnp.mean(h1 * h1, axis=-1, keepdims=True)
    hn = (h1 * lax.rsqrt(ms + EPS) * nw_ref[...]).astype(BF16)
    mlp = jnp.zeros_like(h1)
    for c in range(wup_ref.shape[1] // FF_BLOCK):
        u = _dot(hn, wup_ref[:, c * FF_BLOCK:(c + 1) * FF_BLOCK])
        act = jnp.square(jnp.maximum(u, 0.0)).astype(BF16)
        mlp = mlp + _dot(act, wdn_ref[c * FF_BLOCK:(c + 1) * FF_BLOCK, :])
    out_ref[...] = h1 + mlp


def _out_mlp(o_sb, o_ssd, o_hg, h, w_out, norm_w, w_up, w_down, drop_lead=None):
    rows, d = h.shape
    d_ff = w_up.shape[1]
    const = lambda i: (0, 0)
    operands = (o_sb, o_ssd, o_hg, h)
    if drop_lead is None:
        tm = _row_tile(rows, 512)
        n_views, grid = 1, rows // tm
        row_args = list(operands)
        row_specs = [pl.BlockSpec((tm, t.shape[1]), lambda i: (i, 0)) for t in operands]
        out_spec = pl.BlockSpec((tm, d), lambda i: (i, 0))
        out_shape = jax.ShapeDtypeStruct((rows, d), F32)
    else:
        bsz, length = drop_lead
        chunks_seq = length // CHUNK
        tm = _row_tile(length, 512)
        n_views = tm // CHUNK
        tiles_per_seq = pl.cdiv(length - CHUNK, tm)

        def view(width, m):
            return pl.BlockSpec((None, CHUNK, width), lambda i: (
                (i // tiles_per_seq) * chunks_seq
                + jnp.minimum(1 + (i % tiles_per_seq) * n_views + m, chunks_seq - 1), 0, 0))

        grid = bsz * tiles_per_seq
        row_args = [t.reshape(rows // CHUNK, CHUNK, t.shape[1]) for t in operands for _ in range(n_views)]
        row_specs = [view(t.shape[1], m) for t in operands for m in range(n_views)]
        out_spec = pl.BlockSpec((None, tm, d), lambda i: (i // tiles_per_seq, i % tiles_per_seq, 0))
        out_shape = jax.ShapeDtypeStruct((bsz, length - CHUNK, d), F32)
    weights = (w_out.shape[0] * d + 2 * d * d_ff) * 2
    need = 2 * weights + 2 * tm * (2 * d * 4 + (SB_W + SSD_W + HG_W) * 2) + 6 * tm * FF_BLOCK * 4
    return pl.pallas_call(
        functools.partial(_out_mlp_body, n_views),
        grid=(grid,),
        in_specs=row_specs + [
            pl.BlockSpec(w_out.shape, const),
            pl.BlockSpec((1, d), const),
            pl.BlockSpec(w_up.shape, const),
            pl.BlockSpec(w_down.shape, const),
        ],
        out_specs=out_spec,
        out_shape=out_shape,
        compiler_params=pltpu.CompilerParams(
            dimension_semantics=("parallel",), vmem_limit_bytes=_vmem_limit(need)),
        name="out_mlp",
    )(*row_args, w_out, norm_w, w_up, w_down)


def _block_diag_ones(n, block):
    idx = np.arange(n) // block
    return jnp.asarray(idx[:, None] == idx[None, :], BF16)


def _constants():
    t = np.arange(CHUNK)
    ltri = jnp.asarray(t[None, :] <= t[:, None], BF16)
    later = (t[:, None] > t[None, :])
    usum = jnp.asarray(np.concatenate([later, np.ones((CHUNK, CHUNK), bool)], axis=1), BF16)
    eexp = np.zeros((DT_W, SSD_W), bool)
    for h in range(SSD_HEADS):
        eexp[h, h * HEAD_DIM:(h + 1) * HEAD_DIM] = True
    in_sub = (t[None, :] <= t[:, None]) & (t[None, :] // SUB == t[:, None] // SUB)
    ltri2 = jnp.asarray(np.concatenate([t[None, :] <= t[:, None], in_sub], axis=0), BF16)
    shift = jnp.asarray(np.concatenate([t[None, :] == t[:, None] - k for k in range(1, SSD_CONV)], axis=0), BF16)
    return ltri, ltri2, usum, jnp.asarray(eexp, BF16), shift


def _pack_w_in(w):
    d = w.shape[0]
    dt_lo = 3 * SB_W + 2 * SSD_W + 2 * SSD_BC_W
    dt_hi = dt_lo + SSD_HEADS
    return jnp.concatenate(
        [w[:, :dt_lo], w[:, dt_lo:dt_hi], jnp.zeros((d, DT_W - SSD_HEADS), w.dtype), w[:, dt_hi:]],
        axis=1).astype(BF16)


def _pad_lanes(v, width):
    return jnp.pad(v.astype(F32), (0, width - v.shape[0]))[None, :]


def kernel(x, meta_tokens, hg_lb_logits, norm_mix_w, w_in, sb_q_norm, sb_k_norm, sb_out_norm, ssd_conv_w,
           ssd_conv_b, ssd_dt_bias, ssd_A_log, ssd_D, ssd_norm_w, hg_out_norm, w_out, norm_mlp_w, w_up, w_down):
    bsz, seq, d = x.shape
    depth = w_in.shape[0]
    length = seq + CHUNK
    lead = jnp.concatenate([jnp.zeros((PAD, d), x.dtype), meta_tokens.astype(x.dtype)], axis=0)
    h = x

    probs = jax.nn.softmax(hg_lb_logits.astype(F32), axis=0)
    lbs = jnp.concatenate([jnp.zeros_like(probs[0:1]), jnp.cumsum(probs, axis=0)[:-1]], axis=0)

    ltri, ltri2, usum, eexp, shift = _constants()
    bd256 = _block_diag_ones(SB_W, HEAD_DIM)
    bd128 = _block_diag_ones(LANES, HEAD_DIM)

    for l in range(depth):
        proj = _in_proj(
            h, norm_mix_w[l][None, :], _pack_w_in(w_in[l]),
            jnp.tile(sb_q_norm[l], SB_HEADS)[None, :], jnp.tile(sb_k_norm[l], SB_HEADS)[None, :], bd256,
            _pad_lanes(ssd_dt_bias[l], DT_W), lbs[l][None, :], length, lead if l == 0 else None)
        if l == 0:
            h = proj[-1]
        q, k, v, zg, xbc, dt, hq, hlf, hk, hv, hg = proj[:11]
        o_sb = _sb_attn(q, k, v, sb_out_norm[l].reshape(1, SB_W), usum, bd128, bsz, length)
        o_ssd = _ssd(zg, xbc, dt, ssd_conv_w[l], ssd_conv_b[l][None, :],
                     _pad_lanes(ssd_A_log[l], DT_W), jnp.repeat(ssd_D[l].astype(F32), HEAD_DIM)[None, :],
                     ssd_norm_w[l].reshape(1, SSD_W), ltri, eexp, shift, bsz, length)
        o_hg = _hgrn2(hq, hlf, hk, hv, hg, hg_out_norm[l].reshape(1, HG_W), ltri2, bd256, bsz, length)
        h = _out_mlp(o_sb, o_ssd, o_hg, h, w_out[l].astype(BF16), norm_mlp_w[l][None, :],
                     w_up[l].astype(BF16), w_down[l].astype(BF16), (bsz, length) if l == depth - 1 else None)
    return h
```

```python
import functools

import numpy as np
import jax
import jax.numpy as jnp
from jax import lax
from jax.experimental import pallas as pl
from jax.experimental.pallas import tpu as pltpu

F32 = jnp.float32
BF16 = jnp.bfloat16

N_META = 16
CHUNK = 128
PAD = CHUNK - N_META
HEAD_DIM = 64
SB_HEADS = 4
SB_W = SB_HEADS * HEAD_DIM
SSD_HEADS = 8
SSD_W = SSD_HEADS * HEAD_DIM
SSD_GROUPS = 2
SSD_STATE = 128
SSD_CONV = 4
SSD_BC_W = SSD_GROUPS * SSD_STATE
SSD_CONV_DIM = SSD_W + 2 * SSD_BC_W
HG_HEADS = 4
HG_DK = 64
HG_W = HG_HEADS * HG_DK
EPS = 1e-6
TINY = 1e-30
LOG2E = 1.4426950408889634
HG_SAFE_SPAN = 60.0
SUB = 16
SB_BLOCK = 3 * CHUNK
SB_DEAD_MASS = 150.0
LANES = 128
DT_W = LANES
FF_BLOCK = 1024
ROW_TILE = 512
MIB = 1024 * 1024
VMEM_V7X = 64 * MIB
VMEM_SCOPED_DEFAULT = 32 * MIB
VMEM_REQUEST_MAX = VMEM_V7X - 6 * MIB

_C_Q, _C_K, _C_V = 0, SB_W, 2 * SB_W
_C_Z = 3 * SB_W
_C_XBC = _C_Z + SSD_W
_C_DT = _C_XBC + SSD_CONV_DIM
_C_HQ = _C_DT + DT_W
_C_HF = _C_HQ + HG_W
_C_HI = _C_HF + HG_W
_C_HG = _C_HI + HG_W
D_IN_PACKED = _C_HG + HG_W


def _vmem_limit(need_bytes):
    return int(min(max(need_bytes, VMEM_SCOPED_DEFAULT), VMEM_REQUEST_MAX))


def _seqs_per_step(bsz):
    return next(n for n in (4, 2, 1) if bsz % n == 0)


def _row_tile(rows, target):
    t = min(target, rows)
    while rows % t:
        t -= CHUNK
    return t


def _dot(a, b):
    return jnp.dot(a, b, preferred_element_type=F32)


def _dot_nt(a, b):
    return lax.dot_general(a, b, (((1,), (1,)), ((), ())), preferred_element_type=F32)


def _dot_tn(a, b):
    return lax.dot_general(a, b, (((0,), (0,)), ((), ())), preferred_element_type=F32)


def _split2(x):
    hi = x.astype(BF16)
    lo = (x - hi.astype(F32)).astype(BF16)
    return hi, lo


def _split3(x):
    hi = x.astype(BF16)
    r = x - hi.astype(F32)
    mid = r.astype(BF16)
    lo = (r - mid.astype(F32)).astype(BF16)
    return hi, mid, lo


def _dot_f32_right(x, m):
    hi, lo = _split2(x)
    return _dot(jnp.concatenate([hi, lo], axis=1), jnp.concatenate([m, m], axis=0))


def _dot_f32_left3(m, x):
    hi, mid, lo = _split3(x)
    return _dot(jnp.concatenate([m, m, m], axis=1), jnp.concatenate([hi, mid, lo], axis=0))


def _softplus(x):
    return jnp.maximum(x, 0.0) + jnp.log(1.0 + jnp.exp(-jnp.abs(x)))


def _sigmoid(x):
    return 1.0 / (1.0 + jnp.exp(-x))


def _silu(x):
    return x * _sigmoid(x)


def _in_proj_body(tiles_per_seq, from_input, *refs):
    if from_input:
        n_views = len(refs) - 20
        lead_ref, view_refs, refs = refs[0], refs[1:1 + n_views], refs[1 + n_views:]
        h_out_ref = refs[-1]
    else:
        h_ref, refs = refs[0], refs[1:]
    (nw_ref, w_ref, qn_ref, kn_ref, bd_ref, dtb_ref, lb_ref,
     q_ref, k_ref, v_ref, zg_ref, xbc_ref, dt_ref, hq_ref, hlf_ref, hk_ref, hv_ref, hg_ref) = refs[:18]
    tile = pl.program_id(0) % tiles_per_seq
    if from_input:
        first = jnp.where(tile == 0, lead_ref[...], view_refs[0][...])
        x = jnp.concatenate([first] + [r[...] for r in view_refs[1:]], axis=0)
        h_out_ref[...] = x
    else:
        x = h_ref[...]
    tm = x.shape[0]
    rowi = lax.broadcasted_iota(jnp.int32, (tm, 1), 0)
    valid = tile * tm + rowi >= PAD
    ms = jnp.mean(x * x, axis=-1, keepdims=True)
    hn = (x * lax.rsqrt(ms + EPS) * nw_ref[...]).astype(BF16)

    def seg(lo, width):
        return _dot(hn, w_ref[:, lo:lo + width])

    def head_norm(t, w):
        hms = _dot_f32_right(t * t, bd_ref[...]) * (1.0 / HEAD_DIM)
        return t * lax.rsqrt(hms + EPS) * w

    q = head_norm(seg(_C_Q, SB_W), qn_ref[...])
    q_ref[...] = (q * (HEAD_DIM ** -0.5 * LOG2E)).astype(BF16)
    def live(t):
        return jnp.where(valid, t, 0.0)

    k_ref[...] = live(head_norm(seg(_C_K, SB_W), kn_ref[...])).astype(BF16)
    v_ref[...] = live(seg(_C_V, SB_W)).astype(BF16)
    zg_ref[...] = _silu(seg(_C_Z, SSD_W))
    xbc_ref[...] = live(seg(_C_XBC, SSD_CONV_DIM))
    dt_ref[...] = live(_softplus(seg(_C_DT, DT_W) + dtb_ref[...]))
    hq_ref[...] = _silu(seg(_C_HQ, HG_W))
    lb = lb_ref[...]
    fl = seg(_C_HF, HG_W)
    gate = _sigmoid(fl)
    hlf_ref[...] = live(jnp.log(jnp.maximum(lb + (1.0 - lb) * gate, TINY)))
    hk_ref[...] = live((1.0 - lb) * (1.0 - gate))
    hv_ref[...] = live(seg(_C_HI, HG_W))
    hg_ref[...] = _silu(seg(_C_HG, HG_W))


def _in_proj(h, norm_w, w_packed, qn, kn, bd256, dt_bias, lb, length, lead=None):
    from_input = lead is not None
    d = h.shape[-1]
    rows = h.shape[0] * length if from_input else h.shape[0]
    tm = _row_tile(length, ROW_TILE)
    tiles_per_seq = length // tm
    widths = (SB_W, SB_W, SB_W, SSD_W, SSD_CONV_DIM, DT_W, HG_W, HG_W, HG_W, HG_W, HG_W)
    dtypes = (BF16, BF16, BF16, F32, F32, F32, F32, F32, F32, F32, F32)
    const = lambda i: (0, 0)
    if from_input:
        chunks_in = (length - CHUNK) // CHUNK
        per_tile = tm // CHUNK

        def view(m):
            return pl.BlockSpec((None, CHUNK, d), lambda i: (
                (i // tiles_per_seq) * chunks_in + jnp.maximum((i % tiles_per_seq) * per_tile + m - 1, 0), 0, 0))

        chunks = h.reshape(h.shape[0] * chunks_in, CHUNK, d)
        row_args = [lead] + [chunks] * per_tile
        row_specs = [pl.BlockSpec((CHUNK, d), const)] + [view(m) for m in range(per_tile)]
        widths, dtypes = widths + (d,), dtypes + (F32,)
    else:
        row_args = [h]
        row_specs = [pl.BlockSpec((tm, d), lambda i: (i, 0))]
    out_bytes = sum(w * jnp.dtype(t).itemsize for w, t in zip(widths, dtypes)) * tm
    need = 2 * (tm * d * 4 + d * D_IN_PACKED * 2 + out_bytes) + tm * SSD_CONV_DIM * 4 * 4
    return pl.pallas_call(
        functools.partial(_in_proj_body, tiles_per_seq, from_input),
        grid=(rows // tm,),
        in_specs=row_specs + [
            pl.BlockSpec((1, d), const),
            pl.BlockSpec((d, D_IN_PACKED), const),
            pl.BlockSpec((1, SB_W), const),
            pl.BlockSpec((1, SB_W), const),
            pl.BlockSpec((SB_W, SB_W), const),
            pl.BlockSpec((1, DT_W), const),
            pl.BlockSpec((1, HG_W), const),
        ],
        out_specs=[pl.BlockSpec((tm, w), lambda i: (i, 0)) for w in widths],
        out_shape=[jax.ShapeDtypeStruct((rows, w), t) for w, t in zip(widths, dtypes)],
        compiler_params=pltpu.CompilerParams(
            dimension_semantics=("parallel",), vmem_limit_bytes=_vmem_limit(need)),
        name="in_proj",
    )(*row_args, norm_w, w_packed, qn, kn, bd256, dt_bias, lb)


def _sb_body(q_ref, k_ref, v_ref, onw_ref, usum_ref, bd_ref, o_ref, acc_ref, carry_ref):
    ib = pl.program_id(2)
    nsub = SB_BLOCK // CHUNK
    lane = lax.broadcasted_iota(jnp.int32, (SB_BLOCK, LANES), 1)
    q = q_ref[...]
    zero = jnp.zeros_like(q)
    q2 = jnp.concatenate([jnp.where(lane < HEAD_DIM, q, zero), jnp.where(lane >= HEAD_DIM, q, zero)], axis=0)
    acc_ref[...] = jnp.zeros_like(acc_ref)
    carry_ref[...] = jnp.zeros_like(carry_ref)

    def block_rows(jb):
        return pl.ds(pl.multiple_of(jb * SB_BLOCK, SB_BLOCK), SB_BLOCK)

    def logits(jb):
        return _dot_nt(q2, k_ref[block_rows(jb), :])

    def accumulate(w, jb):
        acc_ref[...] += _dot(w, v_ref[block_rows(jb), :])

    def drop_mass(z):
        return jnp.maximum(z, 0.0) + jnp.log2(1.0 + jnp.exp2(-jnp.abs(z)))

    def weights(z, mask):
        sp = drop_mass(z)
        drop = (sp if mask is None else jnp.where(mask, sp, 0.0)).astype(BF16)
        carry = carry_ref[...]
        ws = [None] * nsub
        for m in reversed(range(nsub)):
            cols = slice(m * CHUNK, (m + 1) * CHUNK)
            sums = _dot(drop[:, cols], usum_ref[...])
            log_w = (z[:, cols] - sp[:, cols]) - sums[:, :CHUNK] - carry
            carry = carry + sums[:, CHUNK:]
            if mask is None:
                w = jnp.exp2(log_w)
            else:
                w = jnp.where(mask[:, cols], jnp.exp2(jnp.where(mask[:, cols], log_w, 0.0)), 0.0)
            ws[m] = w.astype(BF16)
        carry_ref[...] = carry
        return jnp.concatenate(ws, axis=1)

    row = lax.broadcasted_iota(jnp.int32, (SB_BLOCK, SB_BLOCK), 0)
    col = lax.broadcasted_iota(jnp.int32, (SB_BLOCK, SB_BLOCK), 1)
    causal = col < row
    accumulate(weights(logits(ib), jnp.concatenate([causal, causal], axis=0)), ib)

    def live():
        return (jnp.min(carry_ref[...]) < SB_DEAD_MASS).astype(jnp.int32)

    def cond(state):
        jb, alive = state
        return jnp.logical_and(jb >= 0, alive > 0)

    def body(state):
        jb, _ = state
        accumulate(weights(logits(jb), None), jb)
        return jb - 1, live()

    lax.while_loop(cond, body, (ib - 1, live()))

    o = jnp.where(lane < HEAD_DIM, acc_ref[0:SB_BLOCK, :], acc_ref[SB_BLOCK:, :])
    hms = _dot_f32_right(o * o, bd_ref[...]) * (1.0 / HEAD_DIM)
    o_ref[...] = (o * lax.rsqrt(hms + EPS) * onw_ref[...]).astype(BF16)


def _sb_attn(q, k, v, out_norm, usum, bd128, bsz, length):
    assert length % SB_BLOCK == 0
    nb = length // SB_BLOCK
    const = lambda b, p, i: (0, 0)
    need = 2 * (2 * length * LANES * 2) + 24 * 2 * SB_BLOCK * SB_BLOCK * 4
    return pl.pallas_call(
        _sb_body,
        grid=(bsz, SB_W // LANES, nb),
        in_specs=[
            pl.BlockSpec((SB_BLOCK, LANES), lambda b, p, i: (b * nb + i, p)),
            pl.BlockSpec((length, LANES), lambda b, p, i: (b, p)),
            pl.BlockSpec((length, LANES), lambda b, p, i: (b, p)),
            pl.BlockSpec((1, LANES), lambda b, p, i: (0, p)),
            pl.BlockSpec((CHUNK, 2 * CHUNK), const),
            pl.BlockSpec((LANES, LANES), const),
        ],
        out_specs=pl.BlockSpec((SB_BLOCK, LANES), lambda b, p, i: (b * nb + i, p)),
        out_shape=jax.ShapeDtypeStruct((bsz * length, SB_W), BF16),
        scratch_shapes=[pltpu.VMEM((2 * SB_BLOCK, LANES), F32), pltpu.VMEM((2 * SB_BLOCK, CHUNK), F32)],
        compiler_params=pltpu.CompilerParams(
            dimension_semantics=("parallel", "parallel", "arbitrary"), vmem_limit_bytes=_vmem_limit(need)),
        name="sb_attn",
    )(q, k, v, out_norm, usum, bd128)


def _ssd_body(zg_ref, xbc_ref, dt_ref, cw_ref, cb_ref, alog_ref, dexp_ref, nw_ref, ltri_ref, eexp_ref, shift_ref,
              o_ref, tail_ref, st_ref):
    c = pl.program_id(1)

    @pl.when(c == 0)
    def _():
        st_ref[...] = jnp.zeros_like(st_ref)
        tail_ref[...] = jnp.zeros_like(tail_ref)

    for bb in range(zg_ref.shape[0]):
        _ssd_chunk(bb, zg_ref, xbc_ref, dt_ref, cw_ref, cb_ref, alog_ref, dexp_ref, nw_ref, ltri_ref, eexp_ref,
                   shift_ref, o_ref, tail_ref, st_ref)


def _ssd_chunk(bb, zg_ref, xbc_ref, dt_ref, cw_ref, cb_ref, alog_ref, dexp_ref, nw_ref, ltri_ref, eexp_ref,
               shift_ref, o_ref, tail_ref, st_ref):
    heads_per_group = SSD_HEADS // SSD_GROUPS
    group_w = SSD_W // SSD_GROUPS
    taps = SSD_CONV - 1

    u = xbc_ref[bb]
    shifted = _dot(shift_ref[...], u.astype(BF16))
    conv = cb_ref[...] + cw_ref[taps:taps + 1, :] * u
    for kk in range(1, taps + 1):
        conv = conv + cw_ref[taps - kk:taps - kk + 1, :] * shifted[(kk - 1) * CHUNK:kk * CHUNK]
    tail = tail_ref[bb]
    row8 = lax.broadcasted_iota(jnp.int32, (8, 1), 0)
    head_fix = jnp.zeros((8, SSD_CONV_DIM), F32)
    for kk in range(1, taps + 1):
        head_fix = head_fix + cw_ref[taps - kk:taps - kk + 1, :] * jnp.where(row8 < kk, pltpu.roll(tail, kk, 0), 0.0)
    conv = jnp.concatenate([conv[:8] + head_fix, conv[8:]], axis=0)
    tail_ref[bb] = u[CHUNK - 8:]
    act = _silu(conv)
    xs = act[:, :SSD_W]
    bm = act[:, SSD_W:SSD_W + SSD_BC_W].astype(BF16)
    cm = act[:, SSD_W + SSD_BC_W:].astype(BF16)

    dt = dt_ref[bb]
    a = dt * (-jnp.exp(alog_ref[...]))
    acum = _dot_f32_left3(ltri_ref[...], a)
    acum_t = acum.T
    a_last = acum[CHUNK - 1:CHUNK, :]
    per_head = jnp.concatenate([dt, jnp.exp(acum), jnp.exp(a_last - acum)], axis=0)
    expanded = _dot_f32_right(per_head, eexp_ref[...])
    dt_e = expanded[:CHUNK]
    decay_in_e = expanded[CHUNK:2 * CHUNK]
    decay_out_e = expanded[2 * CHUNK:]
    xdt = xs * dt_e
    xdt_b = xdt.astype(BF16)
    xw_b = (xdt * decay_out_e).astype(BF16)

    row = lax.broadcasted_iota(jnp.int32, (CHUNK, CHUNK), 0)
    col = lax.broadcasted_iota(jnp.int32, (CHUNK, CHUNK), 1)
    causal = row >= col
    lane = lax.broadcasted_iota(jnp.int32, (CHUNK, LANES), 1)
    upper_half = lane >= HEAD_DIM

    ys = []
    for g in range(SSD_GROUPS):
        cg = cm[:, g * SSD_STATE:(g + 1) * SSD_STATE]
        bg = bm[:, g * SSD_STATE:(g + 1) * SSD_STATE]
        gcols = slice(g * group_w, (g + 1) * group_w)
        cb = _dot_nt(cg, bg)
        st = st_ref[bb, g]
        y_off = _dot(cg, st.astype(BF16)) * decay_in_e[:, gcols]
        pairs = []
        for pr in range(heads_per_group // 2):
            xp = xdt_b[:, g * group_w + pr * LANES:g * group_w + (pr + 1) * LANES]
            acc = None
            for hh in range(2):
                h = g * heads_per_group + pr * 2 + hh
                seg = acum[:, h:h + 1] - acum_t[h:h + 1, :]
                decay = jnp.where(causal, jnp.exp(jnp.where(causal, seg, 0.0)), 0.0)
                m = (cb * decay).astype(BF16)
                keep = upper_half if hh else jnp.logical_not(upper_half)
                t = _dot(m, jnp.where(keep, xp, jnp.zeros_like(xp)))
                acc = t if acc is None else acc + t
            pairs.append(acc)
        ys.append(jnp.concatenate(pairs, axis=1) + y_off)
        st_ref[bb, g] = st * decay_in_e[CHUNK - 1:CHUNK, gcols] + _dot_tn(bg, xw_b[:, gcols])

    y = jnp.concatenate(ys, axis=1) + xs * dexp_ref[...]
    y = y * zg_ref[bb]
    outs = []
    for g in range(SSD_GROUPS):
        yg = y[:, g * group_w:(g + 1) * group_w]
        gms = jnp.mean(yg * yg, axis=-1, keepdims=True)
        outs.append(yg * lax.rsqrt(gms + EPS) * nw_ref[:, g * group_w:(g + 1) * group_w])
    o_ref[bb] = jnp.concatenate(outs, axis=1).astype(BF16)


def _ssd(zg, xbc, dt, conv_w, conv_b, a_log, d_exp, norm_w, ltri, eexp, shift, bsz, length):
    nc = length // CHUNK
    per = _seqs_per_step(bsz)
    const = lambda b, c: (0, 0)
    rows = lambda b, c: (b, c, 0)
    group_w = SSD_W // SSD_GROUPS
    need = per * (2 * CHUNK * (SSD_W + SSD_CONV_DIM + DT_W) * 4 * 2 + 64 * CHUNK * SSD_CONV_DIM * 4)
    out = pl.pallas_call(
        _ssd_body,
        grid=(bsz // per, nc),
        in_specs=[
            pl.BlockSpec((per, CHUNK, SSD_W), rows),
            pl.BlockSpec((per, CHUNK, SSD_CONV_DIM), rows),
            pl.BlockSpec((per, CHUNK, DT_W), rows),
            pl.BlockSpec((SSD_CONV, SSD_CONV_DIM), const),
            pl.BlockSpec((1, SSD_CONV_DIM), const),
            pl.BlockSpec((1, DT_W), const),
            pl.BlockSpec((1, SSD_W), const),
            pl.BlockSpec((1, SSD_W), const),
            pl.BlockSpec((CHUNK, CHUNK), const),
            pl.BlockSpec((DT_W, SSD_W), const),
            pl.BlockSpec(((SSD_CONV - 1) * CHUNK, CHUNK), const),
        ],
        out_specs=pl.BlockSpec((per, CHUNK, SSD_W), rows),
        out_shape=jax.ShapeDtypeStruct((bsz, length, SSD_W), BF16),
        scratch_shapes=[pltpu.VMEM((per, 8, SSD_CONV_DIM), F32),
                        pltpu.VMEM((per, SSD_GROUPS, SSD_STATE, group_w), F32)],
        compiler_params=pltpu.CompilerParams(
            dimension_semantics=("parallel", "arbitrary"), vmem_limit_bytes=_vmem_limit(need)),
        name="ssd",
    )(zg.reshape(bsz, length, SSD_W), xbc.reshape(bsz, length, SSD_CONV_DIM), dt.reshape(bsz, length, DT_W),
      conv_w, conv_b, a_log, d_exp, norm_w, ltri, eexp, shift)
    return out.reshape(bsz * length, SSD_W)


def _hg_body(q_ref, lf_ref, k_ref, v_ref, g_ref, onw_ref, ltri2_ref, bd_ref,
             o_ref, st_ref, qs_ref, ks_ref, vs_ref, gs_ref, oacc_ref):
    c = pl.program_id(1)
    nsub = CHUNK // SUB
    per = q_ref.shape[0]

    @pl.when(c == 0)
    def _():
        st_ref[...] = jnp.zeros_like(st_ref)

    srow = lax.broadcasted_iota(jnp.int32, (HG_W, HG_W), 0) // HG_DK
    scol = lax.broadcasted_iota(jnp.int32, (HG_W, HG_W), 1) // HG_DK
    erow = lax.broadcasted_iota(jnp.int32, (HG_HEADS * SUB, HG_W), 0) // SUB
    ecol = lax.broadcasted_iota(jnp.int32, (HG_HEADS * SUB, HG_W), 1) // HG_DK
    same_head = erow == ecol

    seqs = []
    for bb in range(per):
        log_f, k, v, q = lf_ref[bb], k_ref[bb], v_ref[bb], q_ref[bb]
        cums = _dot_f32_left3(ltri2_ref[...], log_f)
        gc = cums[:CHUNK]
        span = -cums[CHUNK:]
        g_last = gc[CHUNK - 1:CHUNK, :]

        st = st_ref[bb]
        oacc_ref[bb] = _dot_nt((q * jnp.exp(gc)).astype(BF16), st.astype(BF16))
        k_end = (k * jnp.exp(g_last - gc)).astype(BF16)
        v_b = v.astype(BF16)
        st_ref[bb] = st * jnp.exp(g_last) + jnp.where(srow == scol, _dot_tn(v_b, k_end), 0.0)

        seqs.append((q, k, v, gc, span))

    tri_row = lax.broadcasted_iota(jnp.int32, (SUB, HG_HEADS * SUB), 0)
    tri_col = lax.broadcasted_iota(jnp.int32, (SUB, HG_HEADS * SUB), 1) % SUB
    on_or_before = tri_col <= tri_row

    def key_blocks(bb, q, k, v, gc, own_rows):
        for jb in range(nsub if own_rows else nsub - 1):
            r0, r1 = jb * SUB, (jb + 1) * SUB
            lo = r0 if own_rows else r1
            g_end = gc[r1 - 1:r1, :]
            ke = k[r0:r1] * jnp.exp(g_end - gc[r0:r1])
            ke4 = jnp.where(same_head, jnp.concatenate([ke] * HG_HEADS, axis=0), 0.0).astype(BF16)
            v4 = jnp.where(same_head, jnp.concatenate([v[r0:r1]] * HG_HEADS, axis=0), 0.0).astype(BF16)
            qp = (q[lo:] * jnp.exp(gc[lo:] - g_end)).astype(BF16)
            scores = _dot_nt(qp, ke4)
            if own_rows:
                own = jnp.where(on_or_before, scores[:SUB], 0.0)
                scores = own if jb == nsub - 1 else jnp.concatenate([own, scores[SUB:]], axis=0)
            oacc_ref[bb, lo:, :] += _dot(scores.astype(BF16), v4)

    widest = seqs[0][4]
    for seq in seqs[1:]:
        widest = jnp.maximum(widest, seq[4])
    safe = jnp.max(widest) < HG_SAFE_SPAN

    @pl.when(safe)
    def _():
        for bb, (q, k, v, gc, span) in enumerate(seqs):
            key_blocks(bb, q, k, v, gc, True)

    @pl.when(jnp.logical_not(safe))
    def _():
        rin = lax.broadcasted_iota(jnp.int32, (SUB, 1), 0)
        for bb, (q, k, v, gc, span) in enumerate(seqs):
            key_blocks(bb, q, k, v, gc, False)
            qs_ref[...] = q
            ks_ref[...] = k
            vs_ref[...] = v
            gs_ref[...] = gc

            def diag_block(ib, carry, bb=bb):
                r0 = pl.multiple_of(ib * SUB, SUB)
                qi = qs_ref[pl.ds(r0, SUB), :]
                gi = gs_ref[pl.ds(r0, SUB), :]
                prods = []
                for j in range(SUB):
                    kj = ks_ref[pl.ds(r0 + j, 1), :]
                    gj = gs_ref[pl.ds(r0 + j, 1), :]
                    m = rin >= j
                    prods.append(jnp.where(m, qi * kj * jnp.exp(jnp.where(m, gi - gj, 0.0)), 0.0))
                head_sums = _dot_f32_right(jnp.concatenate(prods, axis=0), bd_ref[...])
                oi = jnp.zeros((SUB, HG_W), F32)
                for j in range(SUB):
                    oi = oi + head_sums[j * SUB:(j + 1) * SUB] * vs_ref[pl.ds(r0 + j, 1), :]
                oacc_ref[bb, pl.ds(r0, SUB), :] += oi
                return carry

            lax.fori_loop(0, nsub, diag_block, 0)

    for bb in range(per):
        o = oacc_ref[bb]
        hms = _dot_f32_right(o * o, bd_ref[...]) * (1.0 / HG_DK)
        o_ref[bb] = (o * lax.rsqrt(hms + EPS) * onw_ref[...] * g_ref[bb]).astype(BF16)


def _hgrn2(hq, hlf, hk, hv, hg, out_norm, ltri2, bd256, bsz, length):
    nc = length // CHUNK
    per = _seqs_per_step(bsz)
    const = lambda b, c: (0, 0)
    rows = lambda b, c: (b, c, 0)
    need = per * 64 * CHUNK * HG_W * 4
    seq3 = lambda t: t.reshape(bsz, length, HG_W)
    out = pl.pallas_call(
        _hg_body,
        grid=(bsz // per, nc),
        in_specs=[pl.BlockSpec((per, CHUNK, HG_W), rows)] * 5 + [
            pl.BlockSpec((1, HG_W), const),
            pl.BlockSpec((2 * CHUNK, CHUNK), const),
            pl.BlockSpec((HG_W, HG_W), const),
        ],
        out_specs=pl.BlockSpec((per, CHUNK, HG_W), rows),
        out_shape=jax.ShapeDtypeStruct((bsz, length, HG_W), BF16),
        scratch_shapes=[pltpu.VMEM((per, HG_W, HG_W), F32)] + [pltpu.VMEM((CHUNK, HG_W), F32)] * 4
        + [pltpu.VMEM((per, CHUNK, HG_W), F32)],
        compiler_params=pltpu.CompilerParams(
            dimension_semantics=("parallel", "arbitrary"), vmem_limit_bytes=_vmem_limit(need)),
        name="hgrn2",
    )(seq3(hq), seq3(hlf), seq3(hk), seq3(hv), seq3(hg), out_norm, ltri2, bd256)
    return out.reshape(bsz * length, HG_W)


def _out_mlp_body(n_views, *refs):
    wo_ref, nw_ref, wup_ref, wdn_ref, out_ref = refs[4 * n_views:]

    def rows(k):
        return jnp.concatenate([r[...] for r in refs[k * n_views:(k + 1) * n_views]], axis=0)

    h1 = (rows(3)
          + _dot(rows(0), wo_ref[0:SB_W, :])
          + _dot(rows(1), wo_ref[SB_W:SB_W + SSD_W, :])
          + _dot(rows(2), wo_ref[SB_W + SSD_W:, :]))
    ms = jnp.mean(h1 * h1, axis=-1, keepdims=True)
    hn = (h1 * lax.rsqrt(ms + EPS) * nw_ref[...]).astype(BF16)
    mlp = jnp.zeros_like(h1)
    for c in range(wup_ref.shape[1] // FF_BLOCK):
        u = _dot(hn, wup_ref[:, c * FF_BLOCK:(c + 1) * FF_BLOCK])
        act = jnp.square(jnp.maximum(u, 0.0)).astype(BF16)
        mlp = mlp + _dot(act, wdn_ref[c * FF_BLOCK:(c + 1) * FF_BLOCK, :])
    out_ref[...] = h1 + mlp


def _out_mlp(o_sb, o_ssd, o_hg, h, w_out, norm_w, w_up, w_down, drop_lead=None):
    rows, d = h.shape
    d_ff = w_up.shape[1]
    const = lambda i: (0, 0)
    operands = (o_sb, o_ssd, o_hg, h)
    if drop_lead is None:
        tm = _row_tile(rows, ROW_TILE)
        n_views, grid = 1, rows // tm
        row_args = list(operands)
        row_specs = [pl.BlockSpec((tm, t.shape[1]), lambda i: (i, 0)) for t in operands]
        out_spec = pl.BlockSpec((tm, d), lambda i: (i, 0))
        out_shape = jax.ShapeDtypeStruct((rows, d), F32)
    else:
        bsz, length = drop_lead
        chunks_seq = length // CHUNK
        tm = _row_tile(length, ROW_TILE)
        n_views = tm // CHUNK
        tiles_per_seq = pl.cdiv(length - CHUNK, tm)

        def view(width, m):
            return pl.BlockSpec((None, CHUNK, width), lambda i: (
                (i // tiles_per_seq) * chunks_seq
                + jnp.minimum(1 + (i % tiles_per_seq) * n_views + m, chunks_seq - 1), 0, 0))

        grid = bsz * tiles_per_seq
        row_args = [t.reshape(rows // CHUNK, CHUNK, t.shape[1]) for t in operands for _ in range(n_views)]
        row_specs = [view(t.shape[1], m) for t in operands for m in range(n_views)]
        out_spec = pl.BlockSpec((None, tm, d), lambda i: (i // tiles_per_seq, i % tiles_per_seq, 0))
        out_shape = jax.ShapeDtypeStruct((bsz, length - CHUNK, d), F32)
    weights = (w_out.shape[0] * d + 2 * d * d_ff) * 2
    need = 2 * weights + 2 * tm * (2 * d * 4 + (SB_W + SSD_W + HG_W) * 2) + 6 * tm * FF_BLOCK * 4
    return pl.pallas_call(
        functools.partial(_out_mlp_body, n_views),
        grid=(grid,),
        in_specs=row_specs + [
            pl.BlockSpec(w_out.shape, const),
            pl.BlockSpec((1, d), const),
            pl.BlockSpec(w_up.shape, const),
            pl.BlockSpec(w_down.shape, const),
        ],
        out_specs=out_spec,
        out_shape=out_shape,
        compiler_params=pltpu.CompilerParams(
            dimension_semantics=("parallel",), vmem_limit_bytes=_vmem_limit(need)),
        name="out_mlp",
    )(*row_args, w_out, norm_w, w_up, w_down)


def _block_diag_ones(n, block):
    idx = np.arange(n) // block
    return jnp.asarray(idx[:, None] == idx[None, :], BF16)


def _constants():
    t = np.arange(CHUNK)
    ltri = jnp.asarray(t[None, :] <= t[:, None], BF16)
    later = (t[:, None] > t[None, :])
    usum = jnp.asarray(np.concatenate([later, np.ones((CHUNK, CHUNK), bool)], axis=1), BF16)
    eexp = np.zeros((DT_W, SSD_W), bool)
    for h in range(SSD_HEADS):
        eexp[h, h * HEAD_DIM:(h + 1) * HEAD_DIM] = True
    in_sub = (t[None, :] <= t[:, None]) & (t[None, :] // SUB == t[:, None] // SUB)
    ltri2 = jnp.asarray(np.concatenate([t[None, :] <= t[:, None], in_sub], axis=0), BF16)
    shift = jnp.asarray(np.concatenate([t[None, :] == t[:, None] - k for k in range(1, SSD_CONV)], axis=0), BF16)
    return ltri, ltri2, usum, jnp.asarray(eexp, BF16), shift


def _pack_w_in(w):
    d = w.shape[0]
    dt_lo = 3 * SB_W + 2 * SSD_W + 2 * SSD_BC_W
    dt_hi = dt_lo + SSD_HEADS
    return jnp.concatenate(
        [w[:, :dt_lo], w[:, dt_lo:dt_hi], jnp.zeros((d, DT_W - SSD_HEADS), w.dtype), w[:, dt_hi:]],
        axis=1).astype(BF16)


def _pad_lanes(v, width):
    return jnp.pad(v.astype(F32), (0, width - v.shape[0]))[None, :]


def kernel(x, meta_tokens, hg_lb_logits, norm_mix_w, w_in, sb_q_norm, sb_k_norm, sb_out_norm, ssd_conv_w,
           ssd_conv_b, ssd_dt_bias, ssd_A_log, ssd_D, ssd_norm_w, hg_out_norm, w_out, norm_mlp_w, w_up, w_down):
    bsz, seq, d = x.shape
    depth = w_in.shape[0]
    length = seq + CHUNK
    lead = jnp.concatenate([jnp.zeros((PAD, d), x.dtype), meta_tokens.astype(x.dtype)], axis=0)
    h = x

    probs = jax.nn.softmax(hg_lb_logits.astype(F32), axis=0)
    lbs = jnp.concatenate([jnp.zeros_like(probs[0:1]), jnp.cumsum(probs, axis=0)[:-1]], axis=0)

    ltri, ltri2, usum, eexp, shift = _constants()
    bd256 = _block_diag_ones(SB_W, HEAD_DIM)
    bd128 = _block_diag_ones(LANES, HEAD_DIM)

    for l in range(depth):
        proj = _in_proj(
            h, norm_mix_w[l][None, :], _pack_w_in(w_in[l]),
            jnp.tile(sb_q_norm[l], SB_HEADS)[None, :], jnp.tile(sb_k_norm[l], SB_HEADS)[None, :], bd256,
            _pad_lanes(ssd_dt_bias[l], DT_W), lbs[l][None, :], length, lead if l == 0 else None)
        if l == 0:
            h = proj[-1]
        q, k, v, zg, xbc, dt, hq, hlf, hk, hv, hg = proj[:11]
        o_sb = _sb_attn(q, k, v, sb_out_norm[l].reshape(1, SB_W), usum, bd128, bsz, length)
        o_ssd = _ssd(zg, xbc, dt, ssd_conv_w[l], ssd_conv_b[l][None, :],
                     _pad_lanes(ssd_A_log[l], DT_W), jnp.repeat(ssd_D[l].astype(F32), HEAD_DIM)[None, :],
                     ssd_norm_w[l].reshape(1, SSD_W), ltri, eexp, shift, bsz, length)
        o_hg = _hgrn2(hq, hlf, hk, hv, hg, hg_out_norm[l].reshape(1, HG_W), ltri2, bd256, bsz, length)
        h = _out_mlp(o_sb, o_ssd, o_hg, h, w_out[l].astype(BF16), norm_mlp_w[l][None, :],
                     w_up[l].astype(BF16), w_down[l].astype(BF16), (bsz, length) if l == depth - 1 else None)
    return h
```

```python
import functools

import numpy as np
import jax
import jax.numpy as jnp
from jax import lax
from jax.experimental import pallas as pl
from jax.experimental.pallas import tpu as pltpu

F32 = jnp.float32
BF16 = jnp.bfloat16

N_META = 16
CHUNK = 128
PAD = CHUNK - N_META
HEAD_DIM = 64
SB_HEADS = 4
SB_W = SB_HEADS * HEAD_DIM
SSD_HEADS = 8
SSD_W = SSD_HEADS * HEAD_DIM
SSD_GROUPS = 2
SSD_STATE = 128
SSD_CONV = 4
SSD_BC_W = SSD_GROUPS * SSD_STATE
SSD_CONV_DIM = SSD_W + 2 * SSD_BC_W
HG_HEADS = 4
HG_DK = 64
HG_W = HG_HEADS * HG_DK
EPS = 1e-6
TINY = 1e-30
LOG2E = 1.4426950408889634
HG_SAFE_SPAN = 60.0
SUB = 16
SB_BLOCK = 3 * CHUNK
SB_DEAD_MASS = 150.0
LANES = 128
SUBLANES = 8
DT_W = LANES
FF_BLOCK = 1024
ROW_TILE = 512
MIB = 1024 * 1024
VMEM_V7X = 64 * MIB
VMEM_SCOPED_DEFAULT = 32 * MIB
VMEM_REQUEST_MAX = VMEM_V7X - 6 * MIB

_C_Q, _C_K, _C_V = 0, SB_W, 2 * SB_W
_C_Z = 3 * SB_W
_C_XBC = _C_Z + SSD_W
_C_DT = _C_XBC + SSD_CONV_DIM
_C_HQ = _C_DT + DT_W
_C_HF = _C_HQ + HG_W
_C_HI = _C_HF + HG_W
_C_HG = _C_HI + HG_W
D_IN_PACKED = _C_HG + HG_W


def _vmem_limit(need_bytes):
    return int(min(max(need_bytes, VMEM_SCOPED_DEFAULT), VMEM_REQUEST_MAX))


def _seqs_per_step(bsz):
    return next(n for n in (4, 2, 1) if bsz % n == 0)


def _row_tile(rows, target):
    t = min(target, rows)
    while rows % t:
        t -= CHUNK
    return t


def _dot(a, b):
    return jnp.dot(a, b, preferred_element_type=F32)


def _dot_nt(a, b):
    return lax.dot_general(a, b, (((1,), (1,)), ((), ())), preferred_element_type=F32)


def _dot_tn(a, b):
    return lax.dot_general(a, b, (((0,), (0,)), ((), ())), preferred_element_type=F32)


def _split2(x):
    hi = x.astype(BF16)
    lo = (x - hi.astype(F32)).astype(BF16)
    return hi, lo


def _split3(x):
    hi = x.astype(BF16)
    r = x - hi.astype(F32)
    mid = r.astype(BF16)
    lo = (r - mid.astype(F32)).astype(BF16)
    return hi, mid, lo


def _dot_f32_right(x, m):
    hi, lo = _split2(x)
    return _dot(jnp.concatenate([hi, lo], axis=1), jnp.concatenate([m, m], axis=0))


def _dot_f32_left3(m, x):
    hi, mid, lo = _split3(x)
    return _dot(jnp.concatenate([m, m, m], axis=1), jnp.concatenate([hi, mid, lo], axis=0))


def _softplus(x):
    return jnp.maximum(x, 0.0) + jnp.log(1.0 + jnp.exp(-jnp.abs(x)))


def _sigmoid(x):
    return 1.0 / (1.0 + jnp.exp(-x))


def _silu(x):
    return x * _sigmoid(x)


def _in_proj_body(tiles_per_seq, from_input, *refs):
    if from_input:
        n_views = len(refs) - 20
        lead_ref, view_refs, refs = refs[0], refs[1:1 + n_views], refs[1 + n_views:]
        h_out_ref = refs[-1]
    else:
        h_ref, refs = refs[0], refs[1:]
    (nw_ref, w_ref, qn_ref, kn_ref, bd_ref, dtb_ref, lb_ref,
     q_ref, k_ref, v_ref, zg_ref, xbc_ref, dt_ref, hq_ref, hlf_ref, hk_ref, hv_ref, hg_ref) = refs[:18]
    tile = pl.program_id(0) % tiles_per_seq
    if from_input:
        first = jnp.where(tile == 0, lead_ref[...], view_refs[0][...])
        x = jnp.concatenate([first] + [r[...] for r in view_refs[1:]], axis=0)
        h_out_ref[...] = x
    else:
        x = h_ref[...]
    tm = x.shape[0]
    rowi = lax.broadcasted_iota(jnp.int32, (tm, 1), 0)
    valid = tile * tm + rowi >= PAD
    ms = jnp.mean(x * x, axis=-1, keepdims=True)
    hn = (x * lax.rsqrt(ms + EPS) * nw_ref[...]).astype(BF16)

    def seg(lo, width):
        return _dot(hn, w_ref[:, lo:lo + width])

    def head_norm(t, w):
        hms = _dot_f32_right(t * t, bd_ref[...]) * (1.0 / HEAD_DIM)
        return t * lax.rsqrt(hms + EPS) * w

    q = head_norm(seg(_C_Q, SB_W), qn_ref[...])
    q_ref[...] = (q * (HEAD_DIM ** -0.5 * LOG2E)).astype(BF16)
    def live(t):
        return jnp.where(valid, t, 0.0)

    k_ref[...] = live(head_norm(seg(_C_K, SB_W), kn_ref[...])).astype(BF16)
    v_ref[...] = live(seg(_C_V, SB_W)).astype(BF16)
    zg_ref[...] = _silu(seg(_C_Z, SSD_W))
    xbc_ref[...] = live(seg(_C_XBC, SSD_CONV_DIM))
    dt_ref[...] = live(_softplus(seg(_C_DT, DT_W) + dtb_ref[...]))
    hq_ref[...] = _silu(seg(_C_HQ, HG_W))
    lb = lb_ref[...]
    fl = seg(_C_HF, HG_W)
    gate = _sigmoid(fl)
    hlf_ref[...] = live(jnp.log(jnp.maximum(lb + (1.0 - lb) * gate, TINY)))
    hk_ref[...] = live((1.0 - lb) * (1.0 - gate))
    hv_ref[...] = live(seg(_C_HI, HG_W))
    hg_ref[...] = _silu(seg(_C_HG, HG_W))


def _in_proj(h, norm_w, w_packed, qn, kn, bd256, dt_bias, lb, length, lead=None):
    from_input = lead is not None
    d = h.shape[-1]
    rows = h.shape[0] * length if from_input else h.shape[0]
    tm = _row_tile(length, ROW_TILE)
    tiles_per_seq = length // tm
    widths = (SB_W, SB_W, SB_W, SSD_W, SSD_CONV_DIM, DT_W, HG_W, HG_W, HG_W, HG_W, HG_W)
    dtypes = (BF16, BF16, BF16, F32, F32, F32, F32, F32, F32, F32, F32)
    const = lambda i: (0, 0)
    if from_input:
        chunks_in = (length - CHUNK) // CHUNK
        per_tile = tm // CHUNK

        def view(m):
            return pl.BlockSpec((None, CHUNK, d), lambda i: (
                (i // tiles_per_seq) * chunks_in + jnp.maximum((i % tiles_per_seq) * per_tile + m - 1, 0), 0, 0))

        chunks = h.reshape(h.shape[0] * chunks_in, CHUNK, d)
        row_args = [lead] + [chunks] * per_tile
        row_specs = [pl.BlockSpec((CHUNK, d), const)] + [view(m) for m in range(per_tile)]
        widths, dtypes = widths + (d,), dtypes + (F32,)
    else:
        row_args = [h]
        row_specs = [pl.BlockSpec((tm, d), lambda i: (i, 0))]
    out_bytes = sum(w * jnp.dtype(t).itemsize for w, t in zip(widths, dtypes)) * tm
    need = 2 * (tm * d * 4 + d * D_IN_PACKED * 2 + out_bytes) + tm * SSD_CONV_DIM * 4 * 4
    return pl.pallas_call(
        functools.partial(_in_proj_body, tiles_per_seq, from_input),
        grid=(rows // tm,),
        in_specs=row_specs + [
            pl.BlockSpec((1, d), const),
            pl.BlockSpec((d, D_IN_PACKED), const),
            pl.BlockSpec((1, SB_W), const),
            pl.BlockSpec((1, SB_W), const),
            pl.BlockSpec((SB_W, SB_W), const),
            pl.BlockSpec((1, DT_W), const),
            pl.BlockSpec((1, HG_W), const),
        ],
        out_specs=[pl.BlockSpec((tm, w), lambda i: (i, 0)) for w in widths],
        out_shape=[jax.ShapeDtypeStruct((rows, w), t) for w, t in zip(widths, dtypes)],
        compiler_params=pltpu.CompilerParams(
            dimension_semantics=("parallel",), vmem_limit_bytes=_vmem_limit(need)),
        name="in_proj",
    )(*row_args, norm_w, w_packed, qn, kn, bd256, dt_bias, lb)


def _sb_body(q_ref, k_ref, v_ref, onw_ref, usum_ref, bd_ref, o_ref, acc_ref, carry_ref):
    ib = pl.program_id(2)
    nsub = SB_BLOCK // CHUNK
    lane = lax.broadcasted_iota(jnp.int32, (SB_BLOCK, LANES), 1)
    q = q_ref[...]
    zero = jnp.zeros_like(q)
    q2 = jnp.concatenate([jnp.where(lane < HEAD_DIM, q, zero), jnp.where(lane >= HEAD_DIM, q, zero)], axis=0)
    def block_rows(jb):
        return pl.ds(pl.multiple_of(jb * SB_BLOCK, SB_BLOCK), SB_BLOCK)

    def logits(jb):
        return _dot_nt(q2, k_ref[block_rows(jb), :])

    def drop_mass(z):
        return jnp.maximum(z, 0.0) + jnp.log2(1.0 + jnp.exp2(-jnp.abs(z)))

    def weights(z, mask):
        sp = drop_mass(z)
        drop = (sp if mask is None else jnp.where(mask, sp, 0.0)).astype(BF16)
        carry = carry_ref[...] if mask is None else None
        ws = [None] * nsub
        for m in reversed(range(nsub)):
            cols = slice(m * CHUNK, (m + 1) * CHUNK)
            sums = _dot(drop[:, cols], usum_ref[...])
            log_w = (z[:, cols] - sp[:, cols]) - sums[:, :CHUNK]
            if carry is not None:
                log_w = log_w - carry
            carry = sums[:, CHUNK:] if carry is None else carry + sums[:, CHUNK:]
            if mask is None:
                w = jnp.exp2(log_w)
            else:
                w = jnp.where(mask[:, cols], jnp.exp2(jnp.where(mask[:, cols], log_w, 0.0)), 0.0)
            ws[m] = w.astype(BF16)
        carry_ref[...] = carry
        return jnp.concatenate(ws, axis=1)

    row = lax.broadcasted_iota(jnp.int32, (SB_BLOCK, SB_BLOCK), 0)
    col = lax.broadcasted_iota(jnp.int32, (SB_BLOCK, SB_BLOCK), 1)
    causal = col < row
    acc_ref[...] = _dot(weights(logits(ib), jnp.concatenate([causal, causal], axis=0)), v_ref[block_rows(ib), :])

    def live():
        return (jnp.min(carry_ref[...]) < SB_DEAD_MASS).astype(jnp.int32)

    def cond(state):
        jb, alive = state
        return jnp.logical_and(jb >= 0, alive > 0)

    def body(state):
        jb, _ = state
        acc_ref[...] += _dot(weights(logits(jb), None), v_ref[block_rows(jb), :])
        return jb - 1, live()

    lax.while_loop(cond, body, (ib - 1, live()))

    o = jnp.where(lane < HEAD_DIM, acc_ref[0:SB_BLOCK, :], acc_ref[SB_BLOCK:, :])
    hms = _dot_f32_right(o * o, bd_ref[...]) * (1.0 / HEAD_DIM)
    o_ref[...] = (o * lax.rsqrt(hms + EPS) * onw_ref[...]).astype(BF16)


def _sb_attn(q, k, v, out_norm, usum, bd128, bsz, length):
    assert length % SB_BLOCK == 0
    nb = length // SB_BLOCK
    const = lambda b, p, i: (0, 0)
    need = 2 * (2 * length * LANES * 2) + 24 * 2 * SB_BLOCK * SB_BLOCK * 4
    return pl.pallas_call(
        _sb_body,
        grid=(bsz, SB_W // LANES, nb),
        in_specs=[
            pl.BlockSpec((SB_BLOCK, LANES), lambda b, p, i: (b * nb + i, p)),
            pl.BlockSpec((length, LANES), lambda b, p, i: (b, p)),
            pl.BlockSpec((length, LANES), lambda b, p, i: (b, p)),
            pl.BlockSpec((1, LANES), lambda b, p, i: (0, p)),
            pl.BlockSpec((CHUNK, 2 * CHUNK), const),
            pl.BlockSpec((LANES, LANES), const),
        ],
        out_specs=pl.BlockSpec((SB_BLOCK, LANES), lambda b, p, i: (b * nb + i, p)),
        out_shape=jax.ShapeDtypeStruct((bsz * length, SB_W), BF16),
        scratch_shapes=[pltpu.VMEM((2 * SB_BLOCK, LANES), F32), pltpu.VMEM((2 * SB_BLOCK, CHUNK), F32)],
        compiler_params=pltpu.CompilerParams(
            dimension_semantics=("parallel", "parallel", "arbitrary"), vmem_limit_bytes=_vmem_limit(need)),
        name="sb_attn",
    )(q, k, v, out_norm, usum, bd128)


def _ssd_body(zg_ref, xbc_ref, dt_ref, cw_ref, cb_ref, alog_ref, dexp_ref, nw_ref, ltri_ref, eexp_ref, shift_ref,
              o_ref, tail_ref, st_ref):
    c = pl.program_id(1)

    @pl.when(c == 0)
    def _():
        st_ref[...] = jnp.zeros_like(st_ref)
        tail_ref[...] = jnp.zeros_like(tail_ref)

    for bb in range(zg_ref.shape[0]):
        _ssd_chunk(bb, zg_ref, xbc_ref, dt_ref, cw_ref, cb_ref, alog_ref, dexp_ref, nw_ref, ltri_ref, eexp_ref,
                   shift_ref, o_ref, tail_ref, st_ref)


def _ssd_chunk(bb, zg_ref, xbc_ref, dt_ref, cw_ref, cb_ref, alog_ref, dexp_ref, nw_ref, ltri_ref, eexp_ref,
               shift_ref, o_ref, tail_ref, st_ref):
    heads_per_group = SSD_HEADS // SSD_GROUPS
    group_w = SSD_W // SSD_GROUPS
    taps = SSD_CONV - 1

    u = xbc_ref[bb]
    shifted = _dot(shift_ref[...], u.astype(BF16))
    conv = cb_ref[...] + cw_ref[taps:taps + 1, :] * u
    for kk in range(1, taps + 1):
        conv = conv + cw_ref[taps - kk:taps - kk + 1, :] * shifted[(kk - 1) * CHUNK:kk * CHUNK]
    tail = tail_ref[bb]
    trow = lax.broadcasted_iota(jnp.int32, (SUBLANES, 1), 0)
    head_fix = jnp.zeros((SUBLANES, SSD_CONV_DIM), F32)
    for kk in range(1, taps + 1):
        head_fix = head_fix + cw_ref[taps - kk:taps - kk + 1, :] * jnp.where(trow < kk, pltpu.roll(tail, kk, 0), 0.0)
    conv = jnp.concatenate([conv[:SUBLANES] + head_fix, conv[SUBLANES:]], axis=0)
    tail_ref[bb] = u[CHUNK - SUBLANES:]
    act = _silu(conv)
    xs = act[:, :SSD_W]
    bm = act[:, SSD_W:SSD_W + SSD_BC_W].astype(BF16)
    cm = act[:, SSD_W + SSD_BC_W:].astype(BF16)

    dt = dt_ref[bb]
    a = dt * (-jnp.exp(alog_ref[...]))
    acum = _dot_f32_left3(ltri_ref[...], a)
    acum_t = acum.T
    a_last = acum[CHUNK - 1:CHUNK, :]
    per_head = jnp.concatenate([dt, jnp.exp(acum), jnp.exp(a_last - acum)], axis=0)
    expanded = _dot_f32_right(per_head, eexp_ref[...])
    dt_e = expanded[:CHUNK]
    decay_in_e = expanded[CHUNK:2 * CHUNK]
    decay_out_e = expanded[2 * CHUNK:]
    xdt = xs * dt_e
    xdt_b = xdt.astype(BF16)
    xw_b = (xdt * decay_out_e).astype(BF16)

    row = lax.broadcasted_iota(jnp.int32, (CHUNK, CHUNK), 0)
    col = lax.broadcasted_iota(jnp.int32, (CHUNK, CHUNK), 1)
    causal = row >= col
    lane = lax.broadcasted_iota(jnp.int32, (CHUNK, LANES), 1)
    upper_half = lane >= HEAD_DIM

    ys = []
    for g in range(SSD_GROUPS):
        cg = cm[:, g * SSD_STATE:(g + 1) * SSD_STATE]
        bg = bm[:, g * SSD_STATE:(g + 1) * SSD_STATE]
        gcols = slice(g * group_w, (g + 1) * group_w)
        cb = _dot_nt(cg, bg)
        st = st_ref[bb, g]
        y_off = _dot(cg, st.astype(BF16)) * decay_in_e[:, gcols]
        pairs = []
        for pr in range(heads_per_group // 2):
            xp = xdt_b[:, g * group_w + pr * LANES:g * group_w + (pr + 1) * LANES]
            acc = None
            for hh in range(2):
                h = g * heads_per_group + pr * 2 + hh
                seg = acum[:, h:h + 1] - acum_t[h:h + 1, :]
                decay = jnp.where(causal, jnp.exp(jnp.where(causal, seg, 0.0)), 0.0)
                m = (cb * decay).astype(BF16)
                keep = upper_half if hh else jnp.logical_not(upper_half)
                t = _dot(m, jnp.where(keep, xp, jnp.zeros_like(xp)))
                acc = t if acc is None else acc + t
            pairs.append(acc)
        ys.append(jnp.concatenate(pairs, axis=1) + y_off)
        st_ref[bb, g] = st * decay_in_e[CHUNK - 1:CHUNK, gcols] + _dot_tn(bg, xw_b[:, gcols])

    y = jnp.concatenate(ys, axis=1) + xs * dexp_ref[...]
    y = y * zg_ref[bb]
    outs = []
    for g in range(SSD_GROUPS):
        yg = y[:, g * group_w:(g + 1) * group_w]
        gms = jnp.mean(yg * yg, axis=-1, keepdims=True)
        outs.append(yg * lax.rsqrt(gms + EPS) * nw_ref[:, g * group_w:(g + 1) * group_w])
    o_ref[bb] = jnp.concatenate(outs, axis=1).astype(BF16)


def _ssd(zg, xbc, dt, conv_w, conv_b, a_log, d_exp, norm_w, ltri, eexp, shift, bsz, length):
    nc = length // CHUNK
    per = _seqs_per_step(bsz)
    const = lambda b, c: (0, 0)
    rows = lambda b, c: (b, c, 0)
    group_w = SSD_W // SSD_GROUPS
    need = per * (2 * CHUNK * (SSD_W + SSD_CONV_DIM + DT_W) * 4 * 2 + 64 * CHUNK * SSD_CONV_DIM * 4)
    out = pl.pallas_call(
        _ssd_body,
        grid=(bsz // per, nc),
        in_specs=[
            pl.BlockSpec((per, CHUNK, SSD_W), rows),
            pl.BlockSpec((per, CHUNK, SSD_CONV_DIM), rows),
            pl.BlockSpec((per, CHUNK, DT_W), rows),
            pl.BlockSpec((SSD_CONV, SSD_CONV_DIM), const),
            pl.BlockSpec((1, SSD_CONV_DIM), const),
            pl.BlockSpec((1, DT_W), const),
            pl.BlockSpec((1, SSD_W), const),
            pl.BlockSpec((1, SSD_W), const),
            pl.BlockSpec((CHUNK, CHUNK), const),
            pl.BlockSpec((DT_W, SSD_W), const),
            pl.BlockSpec(((SSD_CONV - 1) * CHUNK, CHUNK), const),
        ],
        out_specs=pl.BlockSpec((per, CHUNK, SSD_W), rows),
        out_shape=jax.ShapeDtypeStruct((bsz, length, SSD_W), BF16),
        scratch_shapes=[pltpu.VMEM((per, SUBLANES, SSD_CONV_DIM), F32),
                        pltpu.VMEM((per, SSD_GROUPS, SSD_STATE, group_w), F32)],
        compiler_params=pltpu.CompilerParams(
            dimension_semantics=("parallel", "arbitrary"), vmem_limit_bytes=_vmem_limit(need)),
        name="ssd",
    )(zg.reshape(bsz, length, SSD_W), xbc.reshape(bsz, length, SSD_CONV_DIM), dt.reshape(bsz, length, DT_W),
      conv_w, conv_b, a_log, d_exp, norm_w, ltri, eexp, shift)
    return out.reshape(bsz * length, SSD_W)


def _hg_body(q_ref, lf_ref, k_ref, v_ref, g_ref, onw_ref, ltri2_ref, bd_ref,
             o_ref, st_ref, qs_ref, ks_ref, vs_ref, gs_ref, oacc_ref):
    c = pl.program_id(1)
    nsub = CHUNK // SUB
    per = q_ref.shape[0]

    @pl.when(c == 0)
    def _():
        st_ref[...] = jnp.zeros_like(st_ref)

    srow = lax.broadcasted_iota(jnp.int32, (HG_W, HG_W), 0) // HG_DK
    scol = lax.broadcasted_iota(jnp.int32, (HG_W, HG_W), 1) // HG_DK
    erow = lax.broadcasted_iota(jnp.int32, (HG_HEADS * SUB, HG_W), 0) // SUB
    ecol = lax.broadcasted_iota(jnp.int32, (HG_HEADS * SUB, HG_W), 1) // HG_DK
    same_head = erow == ecol

    seqs = []
    for bb in range(per):
        log_f, k, v, q = lf_ref[bb], k_ref[bb], v_ref[bb], q_ref[bb]
        cums = _dot_f32_left3(ltri2_ref[...], log_f)
        gc = cums[:CHUNK]
        span = -cums[CHUNK:]
        g_last = gc[CHUNK - 1:CHUNK, :]

        st = st_ref[bb]
        oacc_ref[bb] = _dot_nt((q * jnp.exp(gc)).astype(BF16), st.astype(BF16))
        k_end = (k * jnp.exp(g_last - gc)).astype(BF16)
        v_b = v.astype(BF16)
        st_ref[bb] = st * jnp.exp(g_last) + jnp.where(srow == scol, _dot_tn(v_b, k_end), 0.0)

        seqs.append((q, k, v, gc, span))

    tri_row = lax.broadcasted_iota(jnp.int32, (SUB, HG_HEADS * SUB), 0)
    tri_col = lax.broadcasted_iota(jnp.int32, (SUB, HG_HEADS * SUB), 1) % SUB
    on_or_before = tri_col <= tri_row

    def key_blocks(bb, q, k, v, gc, own_rows):
        for jb in range(nsub if own_rows else nsub - 1):
            r0, r1 = jb * SUB, (jb + 1) * SUB
            lo = r0 if own_rows else r1
            g_end = gc[r1 - 1:r1, :]
            ke = k[r0:r1] * jnp.exp(g_end - gc[r0:r1])
            ke4 = jnp.where(same_head, jnp.concatenate([ke] * HG_HEADS, axis=0), 0.0).astype(BF16)
            v4 = jnp.where(same_head, jnp.concatenate([v[r0:r1]] * HG_HEADS, axis=0), 0.0).astype(BF16)
            qp = (q[lo:] * jnp.exp(gc[lo:] - g_end)).astype(BF16)
            scores = _dot_nt(qp, ke4)
            if own_rows:
                own = jnp.where(on_or_before, scores[:SUB], 0.0)
                scores = own if jb == nsub - 1 else jnp.concatenate([own, scores[SUB:]], axis=0)
            oacc_ref[bb, lo:, :] += _dot(scores.astype(BF16), v4)

    widest = seqs[0][4]
    for seq in seqs[1:]:
        widest = jnp.maximum(widest, seq[4])
    safe = jnp.max(widest) < HG_SAFE_SPAN

    @pl.when(safe)
    def _():
        for bb, (q, k, v, gc, span) in enumerate(seqs):
            key_blocks(bb, q, k, v, gc, True)

    @pl.when(jnp.logical_not(safe))
    def _():
        rin = lax.broadcasted_iota(jnp.int32, (SUB, 1), 0)
        for bb, (q, k, v, gc, span) in enumerate(seqs):
            key_blocks(bb, q, k, v, gc, False)
            qs_ref[...] = q
            ks_ref[...] = k
            vs_ref[...] = v
            gs_ref[...] = gc

            def diag_block(ib, carry, bb=bb):
                r0 = pl.multiple_of(ib * SUB, SUB)
                qi = qs_ref[pl.ds(r0, SUB), :]
                gi = gs_ref[pl.ds(r0, SUB), :]
                prods = []
                for j in range(SUB):
                    kj = ks_ref[pl.ds(r0 + j, 1), :]
                    gj = gs_ref[pl.ds(r0 + j, 1), :]
                    m = rin >= j
                    prods.append(jnp.where(m, qi * kj * jnp.exp(jnp.where(m, gi - gj, 0.0)), 0.0))
                head_sums = _dot_f32_right(jnp.concatenate(prods, axis=0), bd_ref[...])
                oi = jnp.zeros((SUB, HG_W), F32)
                for j in range(SUB):
                    oi = oi + head_sums[j * SUB:(j + 1) * SUB] * vs_ref[pl.ds(r0 + j, 1), :]
                oacc_ref[bb, pl.ds(r0, SUB), :] += oi
                return carry

            lax.fori_loop(0, nsub, diag_block, 0)

    for bb in range(per):
        o = oacc_ref[bb]
        hms = _dot_f32_right(o * o, bd_ref[...]) * (1.0 / HG_DK)
        o_ref[bb] = (o * lax.rsqrt(hms + EPS) * onw_ref[...] * g_ref[bb]).astype(BF16)


def _hgrn2(hq, hlf, hk, hv, hg, out_norm, ltri2, bd256, bsz, length):
    nc = length // CHUNK
    per = _seqs_per_step(bsz)
    const = lambda b, c: (0, 0)
    rows = lambda b, c: (b, c, 0)
    need = per * 64 * CHUNK * HG_W * 4
    seq3 = lambda t: t.reshape(bsz, length, HG_W)
    out = pl.pallas_call(
        _hg_body,
        grid=(bsz // per, nc),
        in_specs=[pl.BlockSpec((per, CHUNK, HG_W), rows)] * 5 + [
            pl.BlockSpec((1, HG_W), const),
            pl.BlockSpec((2 * CHUNK, CHUNK), const),
            pl.BlockSpec((HG_W, HG_W), const),
        ],
        out_specs=pl.BlockSpec((per, CHUNK, HG_W), rows),
        out_shape=jax.ShapeDtypeStruct((bsz, length, HG_W), BF16),
        scratch_shapes=[pltpu.VMEM((per, HG_W, HG_W), F32)] + [pltpu.VMEM((CHUNK, HG_W), F32)] * 4
        + [pltpu.VMEM((per, CHUNK, HG_W), F32)],
        compiler_params=pltpu.CompilerParams(
            dimension_semantics=("parallel", "arbitrary"), vmem_limit_bytes=_vmem_limit(need)),
        name="hgrn2",
    )(seq3(hq), seq3(hlf), seq3(hk), seq3(hv), seq3(hg), out_norm, ltri2, bd256)
    return out.reshape(bsz * length, HG_W)


def _out_mlp_body(n_views, *refs):
    wo_ref, nw_ref, wup_ref, wdn_ref, out_ref = refs[4 * n_views:]

    def rows(k):
        return jnp.concatenate([r[...] for r in refs[k * n_views:(k + 1) * n_views]], axis=0)

    h1 = (rows(3)
          + _dot(rows(0), wo_ref[0:SB_W, :])
          + _dot(rows(1), wo_ref[SB_W:SB_W + SSD_W, :])
          + _dot(rows(2), wo_ref[SB_W + SSD_W:, :]))
    ms = jnp.mean(h1 * h1, axis=-1, keepdims=True)
    hn = (h1 * lax.rsqrt(ms + EPS) * nw_ref[...]).astype(BF16)
    mlp = jnp.zeros_like(h1)
    for c in range(wup_ref.shape[1] // FF_BLOCK):
        u = _dot(hn, wup_ref[:, c * FF_BLOCK:(c + 1) * FF_BLOCK])
        act = jnp.square(jnp.maximum(u, 0.0)).astype(BF16)
        mlp = mlp + _dot(act, wdn_ref[c * FF_BLOCK:(c + 1) * FF_BLOCK, :])
    out_ref[...] = h1 + mlp


def _out_mlp(o_sb, o_ssd, o_hg, h, w_out, norm_w, w_up, w_down, drop_lead=None):
    rows, d = h.shape
    d_ff = w_up.shape[1]
    const = lambda i: (0, 0)
    operands = (o_sb, o_ssd, o_hg, h)
    if drop_lead is None:
        tm = _row_tile(rows, ROW_TILE)
        n_views, grid = 1, rows // tm
        row_args = list(operands)
        row_specs = [pl.BlockSpec((tm, t.shape[1]), lambda i: (i, 0)) for t in operands]
        out_spec = pl.BlockSpec((tm, d), lambda i: (i, 0))
        out_shape = jax.ShapeDtypeStruct((rows, d), F32)
    else:
        bsz, length = drop_lead
        chunks_seq = length // CHUNK
        tm = _row_tile(length - CHUNK, ROW_TILE)
        n_views = tm // CHUNK
        tiles_per_seq = (length - CHUNK) // tm

        def view(width, m):
            return pl.BlockSpec((None, CHUNK, width), lambda i: (
                (i // tiles_per_seq) * chunks_seq + 1 + (i % tiles_per_seq) * n_views + m, 0, 0))

        grid = bsz * tiles_per_seq
        row_args = [t.reshape(rows // CHUNK, CHUNK, t.shape[1]) for t in operands for _ in range(n_views)]
        row_specs = [view(t.shape[1], m) for t in operands for m in range(n_views)]
        out_spec = pl.BlockSpec((None, tm, d), lambda i: (i // tiles_per_seq, i % tiles_per_seq, 0))
        out_shape = jax.ShapeDtypeStruct((bsz, length - CHUNK, d), F32)
    weights = (w_out.shape[0] * d + 2 * d * d_ff) * 2
    need = 2 * weights + 2 * tm * (2 * d * 4 + (SB_W + SSD_W + HG_W) * 2) + 6 * tm * FF_BLOCK * 4
    return pl.pallas_call(
        functools.partial(_out_mlp_body, n_views),
        grid=(grid,),
        in_specs=row_specs + [
            pl.BlockSpec(w_out.shape, const),
            pl.BlockSpec((1, d), const),
            pl.BlockSpec(w_up.shape, const),
            pl.BlockSpec(w_down.shape, const),
        ],
        out_specs=out_spec,
        out_shape=out_shape,
        compiler_params=pltpu.CompilerParams(
            dimension_semantics=("parallel",), vmem_limit_bytes=_vmem_limit(need)),
        name="out_mlp",
    )(*row_args, w_out, norm_w, w_up, w_down)


def _block_diag_ones(n, block):
    idx = np.arange(n) // block
    return jnp.asarray(idx[:, None] == idx[None, :], BF16)


def _constants():
    t = np.arange(CHUNK)
    ltri = jnp.asarray(t[None, :] <= t[:, None], BF16)
    later = (t[:, None] > t[None, :])
    usum = jnp.asarray(np.concatenate([later, np.ones((CHUNK, CHUNK), bool)], axis=1), BF16)
    eexp = np.zeros((DT_W, SSD_W), bool)
    for h in range(SSD_HEADS):
        eexp[h, h * HEAD_DIM:(h + 1) * HEAD_DIM] = True
    in_sub = (t[None, :] <= t[:, None]) & (t[None, :] // SUB == t[:, None] // SUB)
    ltri2 = jnp.asarray(np.concatenate([t[None, :] <= t[:, None], in_sub], axis=0), BF16)
    shift = jnp.asarray(np.concatenate([t[None, :] == t[:, None] - k for k in range(1, SSD_CONV)], axis=0), BF16)
    return ltri, ltri2, usum, jnp.asarray(eexp, BF16), shift


def _pack_w_in(w):
    d = w.shape[0]
    dt_lo = 3 * SB_W + 2 * SSD_W + 2 * SSD_BC_W
    dt_hi = dt_lo + SSD_HEADS
    return jnp.concatenate(
        [w[:, :dt_lo], w[:, dt_lo:dt_hi], jnp.zeros((d, DT_W - SSD_HEADS), w.dtype), w[:, dt_hi:]],
        axis=1).astype(BF16)


def _pad_lanes(v, width):
    return jnp.pad(v.astype(F32), (0, width - v.shape[0]))[None, :]


def kernel(x, meta_tokens, hg_lb_logits, norm_mix_w, w_in, sb_q_norm, sb_k_norm, sb_out_norm, ssd_conv_w,
           ssd_conv_b, ssd_dt_bias, ssd_A_log, ssd_D, ssd_norm_w, hg_out_norm, w_out, norm_mlp_w, w_up, w_down):
    bsz, seq, d = x.shape
    depth = w_in.shape[0]
    length = seq + CHUNK
    lead = jnp.concatenate([jnp.zeros((PAD, d), x.dtype), meta_tokens.astype(x.dtype)], axis=0)
    h = x

    probs = jax.nn.softmax(hg_lb_logits.astype(F32), axis=0)
    lbs = jnp.concatenate([jnp.zeros_like(probs[0:1]), jnp.cumsum(probs, axis=0)[:-1]], axis=0)

    ltri, ltri2, usum, eexp, shift = _constants()
    bd256 = _block_diag_ones(SB_W, HEAD_DIM)
    bd128 = _block_diag_ones(LANES, HEAD_DIM)

    for l in range(depth):
        proj = _in_proj(
            h, norm_mix_w[l][None, :], _pack_w_in(w_in[l]),
            jnp.tile(sb_q_norm[l], SB_HEADS)[None, :], jnp.tile(sb_k_norm[l], SB_HEADS)[None, :], bd256,
            _pad_lanes(ssd_dt_bias[l], DT_W), lbs[l][None, :], length, lead if l == 0 else None)
        if l == 0:
            h = proj[-1]
        q, k, v, zg, xbc, dt, hq, hlf, hk, hv, hg = proj[:11]
        o_sb = _sb_attn(q, k, v, sb_out_norm[l].reshape(1, SB_W), usum, bd128, bsz, length)
        o_ssd = _ssd(zg, xbc, dt, ssd_conv_w[l], ssd_conv_b[l][None, :],
                     _pad_lanes(ssd_A_log[l], DT_W), jnp.repeat(ssd_D[l].astype(F32), HEAD_DIM)[None, :],
                     ssd_norm_w[l].reshape(1, SSD_W), ltri, eexp, shift, bsz, length)
        o_hg = _hgrn2(hq, hlf, hk, hv, hg, hg_out_norm[l].reshape(1, HG_W), ltri2, bd256, bsz, length)
        h = _out_mlp(o_sb, o_ssd, o_hg, h, w_out[l].astype(BF16), norm_mlp_w[l][None, :],
                     w_up[l].astype(BF16), w_down[l].astype(BF16), (bsz, length) if l == depth - 1 else None)
    return h
```

```python
import functools

import numpy as np
import jax
import jax.numpy as jnp
from jax import lax
from jax.experimental import pallas as pl
from jax.experimental.pallas import tpu as pltpu

F32 = jnp.float32
BF16 = jnp.bfloat16

N_META = 16
CHUNK = 128
PAD = CHUNK - N_META
HEAD_DIM = 64
SB_HEADS = 4
SB_W = SB_HEADS * HEAD_DIM
SSD_HEADS = 8
SSD_W = SSD_HEADS * HEAD_DIM
SSD_GROUPS = 2
SSD_STATE = 128
SSD_CONV = 4
SSD_BC_W = SSD_GROUPS * SSD_STATE
SSD_CONV_DIM = SSD_W + 2 * SSD_BC_W
HG_HEADS = 4
HG_DK = 64
HG_W = HG_HEADS * HG_DK
EPS = 1e-6
TINY = 1e-30
LOG2E = 1.4426950408889634
HG_SAFE_SPAN = 60.0
SUB = 32
SB_BLOCK = 3 * CHUNK
SB_DEAD_MASS = 150.0
LANES = 128
SUBLANES = 8
DT_W = LANES
FF_BLOCK = 1024
ROW_TILE = 512
MIB = 1024 * 1024
VMEM_V7X = 64 * MIB
VMEM_SCOPED_DEFAULT = 32 * MIB
VMEM_REQUEST_MAX = VMEM_V7X - 6 * MIB

_C_Q, _C_K, _C_V = 0, SB_W, 2 * SB_W
_C_Z = 3 * SB_W
_C_XBC = _C_Z + SSD_W
_C_DT = _C_XBC + SSD_CONV_DIM
_C_HQ = _C_DT + DT_W
_C_HF = _C_HQ + HG_W
_C_HI = _C_HF + HG_W
_C_HG = _C_HI + HG_W
D_IN_PACKED = _C_HG + HG_W


def _vmem_limit(need_bytes):
    return int(min(max(need_bytes, VMEM_SCOPED_DEFAULT), VMEM_REQUEST_MAX))


def _seqs_per_step(bsz):
    return next(n for n in (4, 2, 1) if bsz % n == 0)


def _row_tile(rows, target):
    t = min(target, rows)
    while rows % t:
        t -= CHUNK
    return t


def _dot(a, b):
    return jnp.dot(a, b, preferred_element_type=F32)


def _dot_nt(a, b):
    return lax.dot_general(a, b, (((1,), (1,)), ((), ())), preferred_element_type=F32)


def _dot_tn(a, b):
    return lax.dot_general(a, b, (((0,), (0,)), ((), ())), preferred_element_type=F32)


def _split2(x):
    hi = x.astype(BF16)
    lo = (x - hi.astype(F32)).astype(BF16)
    return hi, lo


def _split3(x):
    hi = x.astype(BF16)
    r = x - hi.astype(F32)
    mid = r.astype(BF16)
    lo = (r - mid.astype(F32)).astype(BF16)
    return hi, mid, lo


def _dot_f32_right(x, m):
    hi, lo = _split2(x)
    return _dot(jnp.concatenate([hi, lo], axis=1), jnp.concatenate([m, m], axis=0))


def _dot_f32_left3(m, x):
    hi, mid, lo = _split3(x)
    return _dot(jnp.concatenate([m, m, m], axis=1), jnp.concatenate([hi, mid, lo], axis=0))


def _softplus(x):
    return jnp.maximum(x, 0.0) + jnp.log(1.0 + jnp.exp(-jnp.abs(x)))


def _sigmoid(x):
    return 1.0 / (1.0 + jnp.exp(-x))


def _silu(x):
    return x * _sigmoid(x)


def _in_proj_body(tiles_per_seq, from_input, *refs):
    if from_input:
        n_views = len(refs) - 20
        lead_ref, view_refs, refs = refs[0], refs[1:1 + n_views], refs[1 + n_views:]
        h_out_ref = refs[-1]
    else:
        h_ref, refs = refs[0], refs[1:]
    (nw_ref, w_ref, qn_ref, kn_ref, bd_ref, dtb_ref, lb_ref,
     q_ref, k_ref, v_ref, zg_ref, xbc_ref, dt_ref, hq_ref, hlf_ref, hk_ref, hv_ref, hg_ref) = refs[:18]
    tile = pl.program_id(0) % tiles_per_seq
    if from_input:
        first = jnp.where(tile == 0, lead_ref[...], view_refs[0][...])
        x = jnp.concatenate([first] + [r[...] for r in view_refs[1:]], axis=0)
        h_out_ref[...] = x
    else:
        x = h_ref[...]
    tm = x.shape[0]
    rowi = lax.broadcasted_iota(jnp.int32, (tm, 1), 0)
    valid = tile * tm + rowi >= PAD
    ms = jnp.mean(x * x, axis=-1, keepdims=True)
    hn = (x * lax.rsqrt(ms + EPS) * nw_ref[...]).astype(BF16)

    def seg(lo, width):
        return _dot(hn, w_ref[:, lo:lo + width])

    def head_norm(t, w):
        hms = _dot_f32_right(t * t, bd_ref[...]) * (1.0 / HEAD_DIM)
        return t * lax.rsqrt(hms + EPS) * w

    q = head_norm(seg(_C_Q, SB_W), qn_ref[...])
    q_ref[...] = (q * (HEAD_DIM ** -0.5 * LOG2E)).astype(BF16)
    def live(t):
        return jnp.where(valid, t, 0.0)

    k_ref[...] = live(head_norm(seg(_C_K, SB_W), kn_ref[...])).astype(BF16)
    v_ref[...] = live(seg(_C_V, SB_W)).astype(BF16)
    zg_ref[...] = _silu(seg(_C_Z, SSD_W))
    xbc_ref[...] = live(seg(_C_XBC, SSD_CONV_DIM))
    dt_ref[...] = live(_softplus(seg(_C_DT, DT_W) + dtb_ref[...]))
    hq_ref[...] = _silu(seg(_C_HQ, HG_W))
    lb = lb_ref[...]
    fl = seg(_C_HF, HG_W)
    gate = _sigmoid(fl)
    hlf_ref[...] = live(jnp.log(jnp.maximum(lb + (1.0 - lb) * gate, TINY)))
    hk_ref[...] = live((1.0 - lb) * (1.0 - gate))
    hv_ref[...] = live(seg(_C_HI, HG_W))
    hg_ref[...] = _silu(seg(_C_HG, HG_W))


def _in_proj(h, norm_w, w_packed, qn, kn, bd256, dt_bias, lb, length, lead=None):
    from_input = lead is not None
    d = h.shape[-1]
    rows = h.shape[0] * length if from_input else h.shape[0]
    tm = _row_tile(length, ROW_TILE)
    tiles_per_seq = length // tm
    widths = (SB_W, SB_W, SB_W, SSD_W, SSD_CONV_DIM, DT_W, HG_W, HG_W, HG_W, HG_W, HG_W)
    dtypes = (BF16, BF16, BF16, F32, F32, F32, F32, F32, F32, F32, F32)
    const = lambda i: (0, 0)
    if from_input:
        chunks_in = (length - CHUNK) // CHUNK
        per_tile = tm // CHUNK

        def view(m):
            return pl.BlockSpec((None, CHUNK, d), lambda i: (
                (i // tiles_per_seq) * chunks_in + jnp.maximum((i % tiles_per_seq) * per_tile + m - 1, 0), 0, 0))

        chunks = h.reshape(h.shape[0] * chunks_in, CHUNK, d)
        row_args = [lead] + [chunks] * per_tile
        row_specs = [pl.BlockSpec((CHUNK, d), const)] + [view(m) for m in range(per_tile)]
        widths, dtypes = widths + (d,), dtypes + (F32,)
    else:
        row_args = [h]
        row_specs = [pl.BlockSpec((tm, d), lambda i: (i, 0))]
    out_bytes = sum(w * jnp.dtype(t).itemsize for w, t in zip(widths, dtypes)) * tm
    need = 2 * (tm * d * 4 + d * D_IN_PACKED * 2 + out_bytes) + tm * SSD_CONV_DIM * 4 * 4
    return pl.pallas_call(
        functools.partial(_in_proj_body, tiles_per_seq, from_input),
        grid=(rows // tm,),
        in_specs=row_specs + [
            pl.BlockSpec((1, d), const),
            pl.BlockSpec((d, D_IN_PACKED), const),
            pl.BlockSpec((1, SB_W), const),
            pl.BlockSpec((1, SB_W), const),
            pl.BlockSpec((SB_W, SB_W), const),
            pl.BlockSpec((1, DT_W), const),
            pl.BlockSpec((1, HG_W), const),
        ],
        out_specs=[pl.BlockSpec((tm, w), lambda i: (i, 0)) for w in widths],
        out_shape=[jax.ShapeDtypeStruct((rows, w), t) for w, t in zip(widths, dtypes)],
        compiler_params=pltpu.CompilerParams(
            dimension_semantics=("parallel",), vmem_limit_bytes=_vmem_limit(need)),
        name="in_proj",
    )(*row_args, norm_w, w_packed, qn, kn, bd256, dt_bias, lb)


def _sb_body(q_ref, k_ref, v_ref, onw_ref, usum_ref, bd_ref, o_ref, acc_ref, carry_ref):
    ib = pl.program_id(2)
    nsub = SB_BLOCK // CHUNK
    lane = lax.broadcasted_iota(jnp.int32, (SB_BLOCK, LANES), 1)
    q = q_ref[...]
    zero = jnp.zeros_like(q)
    q2 = jnp.concatenate([jnp.where(lane < HEAD_DIM, q, zero), jnp.where(lane >= HEAD_DIM, q, zero)], axis=0)
    def block_rows(jb):
        return pl.ds(pl.multiple_of(jb * SB_BLOCK, SB_BLOCK), SB_BLOCK)

    def logits(jb):
        return _dot_nt(q2, k_ref[block_rows(jb), :])

    def drop_mass(z):
        return jnp.maximum(z, 0.0) + jnp.log2(1.0 + jnp.exp2(-jnp.abs(z)))

    def weights(z, mask):
        sp = drop_mass(z)
        drop = (sp if mask is None else jnp.where(mask, sp, 0.0)).astype(BF16)
        carry = carry_ref[...] if mask is None else None
        ws = [None] * nsub
        for m in reversed(range(nsub)):
            cols = slice(m * CHUNK, (m + 1) * CHUNK)
            sums = _dot(drop[:, cols], usum_ref[...])
            log_w = (z[:, cols] - sp[:, cols]) - sums[:, :CHUNK]
            if carry is not None:
                log_w = log_w - carry
            carry = sums[:, CHUNK:] if carry is None else carry + sums[:, CHUNK:]
            if mask is None:
                w = jnp.exp2(log_w)
            else:
                w = jnp.where(mask[:, cols], jnp.exp2(jnp.where(mask[:, cols], log_w, 0.0)), 0.0)
            ws[m] = w.astype(BF16)
        carry_ref[...] = carry
        return jnp.concatenate(ws, axis=1)

    row = lax.broadcasted_iota(jnp.int32, (SB_BLOCK, SB_BLOCK), 0)
    col = lax.broadcasted_iota(jnp.int32, (SB_BLOCK, SB_BLOCK), 1)
    causal = col < row
    acc_ref[...] = _dot(weights(logits(ib), jnp.concatenate([causal, causal], axis=0)), v_ref[block_rows(ib), :])

    def live():
        return (jnp.min(carry_ref[...]) < SB_DEAD_MASS).astype(jnp.int32)

    def cond(state):
        jb, alive = state
        return jnp.logical_and(jb >= 0, alive > 0)

    def body(state):
        jb, _ = state
        acc_ref[...] += _dot(weights(logits(jb), None), v_ref[block_rows(jb), :])
        return jb - 1, live()

    lax.while_loop(cond, body, (ib - 1, live()))

    o = jnp.where(lane < HEAD_DIM, acc_ref[0:SB_BLOCK, :], acc_ref[SB_BLOCK:, :])
    hms = _dot_f32_right(o * o, bd_ref[...]) * (1.0 / HEAD_DIM)
    o_ref[...] = (o * lax.rsqrt(hms + EPS) * onw_ref[...]).astype(BF16)


def _sb_attn(q, k, v, out_norm, usum, bd128, bsz, length):
    assert length % SB_BLOCK == 0
    nb = length // SB_BLOCK
    const = lambda b, p, i: (0, 0)
    need = 2 * (2 * length * LANES * 2) + 24 * 2 * SB_BLOCK * SB_BLOCK * 4
    return pl.pallas_call(
        _sb_body,
        grid=(bsz, SB_W // LANES, nb),
        in_specs=[
            pl.BlockSpec((SB_BLOCK, LANES), lambda b, p, i: (b * nb + i, p)),
            pl.BlockSpec((length, LANES), lambda b, p, i: (b, p)),
            pl.BlockSpec((length, LANES), lambda b, p, i: (b, p)),
            pl.BlockSpec((1, LANES), lambda b, p, i: (0, p)),
            pl.BlockSpec((CHUNK, 2 * CHUNK), const),
            pl.BlockSpec((LANES, LANES), const),
        ],
        out_specs=pl.BlockSpec((SB_BLOCK, LANES), lambda b, p, i: (b * nb + i, p)),
        out_shape=jax.ShapeDtypeStruct((bsz * length, SB_W), BF16),
        scratch_shapes=[pltpu.VMEM((2 * SB_BLOCK, LANES), F32), pltpu.VMEM((2 * SB_BLOCK, CHUNK), F32)],
        compiler_params=pltpu.CompilerParams(
            dimension_semantics=("parallel", "parallel", "arbitrary"), vmem_limit_bytes=_vmem_limit(need)),
        name="sb_attn",
    )(q, k, v, out_norm, usum, bd128)


def _ssd_body(zg_ref, xbc_ref, dt_ref, cw_ref, cb_ref, alog_ref, dexp_ref, nw_ref, ltri_ref, eexp_ref, shift_ref,
              o_ref, tail_ref, st_ref):
    c = pl.program_id(1)

    @pl.when(c == 0)
    def _():
        st_ref[...] = jnp.zeros_like(st_ref)
        tail_ref[...] = jnp.zeros_like(tail_ref)

    for bb in range(zg_ref.shape[0]):
        _ssd_chunk(bb, zg_ref, xbc_ref, dt_ref, cw_ref, cb_ref, alog_ref, dexp_ref, nw_ref, ltri_ref, eexp_ref,
                   shift_ref, o_ref, tail_ref, st_ref)


def _ssd_chunk(bb, zg_ref, xbc_ref, dt_ref, cw_ref, cb_ref, alog_ref, dexp_ref, nw_ref, ltri_ref, eexp_ref,
               shift_ref, o_ref, tail_ref, st_ref):
    heads_per_group = SSD_HEADS // SSD_GROUPS
    group_w = SSD_W // SSD_GROUPS
    taps = SSD_CONV - 1

    u = xbc_ref[bb]
    shifted = _dot(shift_ref[...], u.astype(BF16))
    conv = cb_ref[...] + cw_ref[taps:taps + 1, :] * u
    for kk in range(1, taps + 1):
        conv = conv + cw_ref[taps - kk:taps - kk + 1, :] * shifted[(kk - 1) * CHUNK:kk * CHUNK]
    tail = tail_ref[bb]
    trow = lax.broadcasted_iota(jnp.int32, (SUBLANES, 1), 0)
    head_fix = jnp.zeros((SUBLANES, SSD_CONV_DIM), F32)
    for kk in range(1, taps + 1):
        head_fix = head_fix + cw_ref[taps - kk:taps - kk + 1, :] * jnp.where(trow < kk, pltpu.roll(tail, kk, 0), 0.0)
    conv = jnp.concatenate([conv[:SUBLANES] + head_fix, conv[SUBLANES:]], axis=0)
    tail_ref[bb] = u[CHUNK - SUBLANES:]
    act = _silu(conv)
    xs = act[:, :SSD_W]
    bm = act[:, SSD_W:SSD_W + SSD_BC_W].astype(BF16)
    cm = act[:, SSD_W + SSD_BC_W:].astype(BF16)

    dt = dt_ref[bb]
    a = dt * (-jnp.exp(alog_ref[...]))
    acum = _dot_f32_left3(ltri_ref[...], a)
    acum_t = acum.T
    a_last = acum[CHUNK - 1:CHUNK, :]
    per_head = jnp.concatenate([dt, jnp.exp(acum), jnp.exp(a_last - acum)], axis=0)
    expanded = _dot_f32_right(per_head, eexp_ref[...])
    dt_e = expanded[:CHUNK]
    decay_in_e = expanded[CHUNK:2 * CHUNK]
    decay_out_e = expanded[2 * CHUNK:]
    xdt = xs * dt_e
    xdt_b = xdt.astype(BF16)
    xw_b = (xdt * decay_out_e).astype(BF16)

    row = lax.broadcasted_iota(jnp.int32, (CHUNK, CHUNK), 0)
    col = lax.broadcasted_iota(jnp.int32, (CHUNK, CHUNK), 1)
    causal = row >= col
    lane = lax.broadcasted_iota(jnp.int32, (CHUNK, LANES), 1)
    upper_half = lane >= HEAD_DIM

    ys = []
    for g in range(SSD_GROUPS):
        cg = cm[:, g * SSD_STATE:(g + 1) * SSD_STATE]
        bg = bm[:, g * SSD_STATE:(g + 1) * SSD_STATE]
        gcols = slice(g * group_w, (g + 1) * group_w)
        cb = _dot_nt(cg, bg)
        st = st_ref[bb, g]
        y_off = _dot(cg, st.astype(BF16)) * decay_in_e[:, gcols]
        pairs = []
        for pr in range(heads_per_group // 2):
            xp = xdt_b[:, g * group_w + pr * LANES:g * group_w + (pr + 1) * LANES]
            acc = None
            for hh in range(2):
                h = g * heads_per_group + pr * 2 + hh
                seg = acum[:, h:h + 1] - acum_t[h:h + 1, :]
                decay = jnp.where(causal, jnp.exp(jnp.where(causal, seg, 0.0)), 0.0)
                m = (cb * decay).astype(BF16)
                keep = upper_half if hh else jnp.logical_not(upper_half)
                t = _dot(m, jnp.where(keep, xp, jnp.zeros_like(xp)))
                acc = t if acc is None else acc + t
            pairs.append(acc)
        ys.append(jnp.concatenate(pairs, axis=1) + y_off)
        st_ref[bb, g] = st * decay_in_e[CHUNK - 1:CHUNK, gcols] + _dot_tn(bg, xw_b[:, gcols])

    y = jnp.concatenate(ys, axis=1) + xs * dexp_ref[...]
    y = y * zg_ref[bb]
    outs = []
    for g in range(SSD_GROUPS):
        yg = y[:, g * group_w:(g + 1) * group_w]
        gms = jnp.mean(yg * yg, axis=-1, keepdims=True)
        outs.append(yg * lax.rsqrt(gms + EPS) * nw_ref[:, g * group_w:(g + 1) * group_w])
    o_ref[bb] = jnp.concatenate(outs, axis=1).astype(BF16)


def _ssd(zg, xbc, dt, conv_w, conv_b, a_log, d_exp, norm_w, ltri, eexp, shift, bsz, length):
    nc = length // CHUNK
    per = _seqs_per_step(bsz)
    const = lambda b, c: (0, 0)
    rows = lambda b, c: (b, c, 0)
    group_w = SSD_W // SSD_GROUPS
    need = per * (2 * CHUNK * (SSD_W + SSD_CONV_DIM + DT_W) * 4 * 2 + 64 * CHUNK * SSD_CONV_DIM * 4)
    out = pl.pallas_call(
        _ssd_body,
        grid=(bsz // per, nc),
        in_specs=[
            pl.BlockSpec((per, CHUNK, SSD_W), rows),
            pl.BlockSpec((per, CHUNK, SSD_CONV_DIM), rows),
            pl.BlockSpec((per, CHUNK, DT_W), rows),
            pl.BlockSpec((SSD_CONV, SSD_CONV_DIM), const),
            pl.BlockSpec((1, SSD_CONV_DIM), const),
            pl.BlockSpec((1, DT_W), const),
            pl.BlockSpec((1, SSD_W), const),
            pl.BlockSpec((1, SSD_W), const),
            pl.BlockSpec((CHUNK, CHUNK), const),
            pl.BlockSpec((DT_W, SSD_W), const),
            pl.BlockSpec(((SSD_CONV - 1) * CHUNK, CHUNK), const),
        ],
        out_specs=pl.BlockSpec((per, CHUNK, SSD_W), rows),
        out_shape=jax.ShapeDtypeStruct((bsz, length, SSD_W), BF16),
        scratch_shapes=[pltpu.VMEM((per, SUBLANES, SSD_CONV_DIM), F32),
                        pltpu.VMEM((per, SSD_GROUPS, SSD_STATE, group_w), F32)],
        compiler_params=pltpu.CompilerParams(
            dimension_semantics=("parallel", "arbitrary"), vmem_limit_bytes=_vmem_limit(need)),
        name="ssd",
    )(zg.reshape(bsz, length, SSD_W), xbc.reshape(bsz, length, SSD_CONV_DIM), dt.reshape(bsz, length, DT_W),
      conv_w, conv_b, a_log, d_exp, norm_w, ltri, eexp, shift)
    return out.reshape(bsz * length, SSD_W)


def _hg_body(q_ref, lf_ref, k_ref, v_ref, g_ref, onw_ref, ltri2_ref, bd_ref,
             o_ref, st_ref, qs_ref, ks_ref, vs_ref, gs_ref, oacc_ref):
    c = pl.program_id(1)
    nsub = CHUNK // SUB
    per = q_ref.shape[0]

    @pl.when(c == 0)
    def _():
        st_ref[...] = jnp.zeros_like(st_ref)

    srow = lax.broadcasted_iota(jnp.int32, (HG_W, HG_W), 0) // HG_DK
    scol = lax.broadcasted_iota(jnp.int32, (HG_W, HG_W), 1) // HG_DK
    erow = lax.broadcasted_iota(jnp.int32, (HG_HEADS * SUB, HG_W), 0) // SUB
    ecol = lax.broadcasted_iota(jnp.int32, (HG_HEADS * SUB, HG_W), 1) // HG_DK
    same_head = erow == ecol

    seqs = []
    for bb in range(per):
        log_f, k, v, q = lf_ref[bb], k_ref[bb], v_ref[bb], q_ref[bb]
        cums = _dot_f32_left3(ltri2_ref[...], log_f)
        gc = cums[:CHUNK]
        span = -cums[CHUNK:]
        g_last = gc[CHUNK - 1:CHUNK, :]

        st = st_ref[bb]
        oacc_ref[bb] = _dot_nt((q * jnp.exp(gc)).astype(BF16), st.astype(BF16))
        k_end = (k * jnp.exp(g_last - gc)).astype(BF16)
        v_b = v.astype(BF16)
        st_ref[bb] = st * jnp.exp(g_last) + jnp.where(srow == scol, _dot_tn(v_b, k_end), 0.0)

        seqs.append((q, k, v, gc, span))

    tri_row = lax.broadcasted_iota(jnp.int32, (SUB, HG_HEADS * SUB), 0)
    tri_col = lax.broadcasted_iota(jnp.int32, (SUB, HG_HEADS * SUB), 1) % SUB
    on_or_before = tri_col <= tri_row

    def key_blocks(bb, q, k, v, gc, own_rows):
        for jb in range(nsub if own_rows else nsub - 1):
            r0, r1 = jb * SUB, (jb + 1) * SUB
            lo = r0 if own_rows else r1
            g_end = gc[r1 - 1:r1, :]
            ke = k[r0:r1] * jnp.exp(g_end - gc[r0:r1])
            ke4 = jnp.where(same_head, jnp.concatenate([ke] * HG_HEADS, axis=0), 0.0).astype(BF16)
            v4 = jnp.where(same_head, jnp.concatenate([v[r0:r1]] * HG_HEADS, axis=0), 0.0).astype(BF16)
            qp = (q[lo:] * jnp.exp(gc[lo:] - g_end)).astype(BF16)
            scores = _dot_nt(qp, ke4)
            if own_rows:
                own = jnp.where(on_or_before, scores[:SUB], 0.0)
                scores = own if jb == nsub - 1 else jnp.concatenate([own, scores[SUB:]], axis=0)
            oacc_ref[bb, lo:, :] += _dot(scores.astype(BF16), v4)

    widest = seqs[0][4]
    for seq in seqs[1:]:
        widest = jnp.maximum(widest, seq[4])
    safe = jnp.max(widest) < HG_SAFE_SPAN

    @pl.when(safe)
    def _():
        for bb, (q, k, v, gc, span) in enumerate(seqs):
            key_blocks(bb, q, k, v, gc, True)

    @pl.when(jnp.logical_not(safe))
    def _():
        rin = lax.broadcasted_iota(jnp.int32, (SUB, 1), 0)
        for bb, (q, k, v, gc, span) in enumerate(seqs):
            key_blocks(bb, q, k, v, gc, False)
            qs_ref[...] = q
            ks_ref[...] = k
            vs_ref[...] = v
            gs_ref[...] = gc

            def diag_block(ib, carry, bb=bb):
                r0 = pl.multiple_of(ib * SUB, SUB)
                qi = qs_ref[pl.ds(r0, SUB), :]
                gi = gs_ref[pl.ds(r0, SUB), :]
                prods = []
                for j in range(SUB):
                    kj = ks_ref[pl.ds(r0 + j, 1), :]
                    gj = gs_ref[pl.ds(r0 + j, 1), :]
                    m = rin >= j
                    prods.append(jnp.where(m, qi * kj * jnp.exp(jnp.where(m, gi - gj, 0.0)), 0.0))
                head_sums = _dot_f32_right(jnp.concatenate(prods, axis=0), bd_ref[...])
                oi = jnp.zeros((SUB, HG_W), F32)
                for j in range(SUB):
                    oi = oi + head_sums[j * SUB:(j + 1) * SUB] * vs_ref[pl.ds(r0 + j, 1), :]
                oacc_ref[bb, pl.ds(r0, SUB), :] += oi
                return carry

            lax.fori_loop(0, nsub, diag_block, 0)

    for bb in range(per):
        o = oacc_ref[bb]
        hms = _dot_f32_right(o * o, bd_ref[...]) * (1.0 / HG_DK)
        o_ref[bb] = (o * lax.rsqrt(hms + EPS) * onw_ref[...] * g_ref[bb]).astype(BF16)


def _hgrn2(hq, hlf, hk, hv, hg, out_norm, ltri2, bd256, bsz, length):
    nc = length // CHUNK
    per = _seqs_per_step(bsz)
    const = lambda b, c: (0, 0)
    rows = lambda b, c: (b, c, 0)
    need = per * 64 * CHUNK * HG_W * 4
    seq3 = lambda t: t.reshape(bsz, length, HG_W)
    out = pl.pallas_call(
        _hg_body,
        grid=(bsz // per, nc),
        in_specs=[pl.BlockSpec((per, CHUNK, HG_W), rows)] * 5 + [
            pl.BlockSpec((1, HG_W), const),
            pl.BlockSpec((2 * CHUNK, CHUNK), const),
            pl.BlockSpec((HG_W, HG_W), const),
        ],
        out_specs=pl.BlockSpec((per, CHUNK, HG_W), rows),
        out_shape=jax.ShapeDtypeStruct((bsz, length, HG_W), BF16),
        scratch_shapes=[pltpu.VMEM((per, HG_W, HG_W), F32)] + [pltpu.VMEM((CHUNK, HG_W), F32)] * 4
        + [pltpu.VMEM((per, CHUNK, HG_W), F32)],
        compiler_params=pltpu.CompilerParams(
            dimension_semantics=("parallel", "arbitrary"), vmem_limit_bytes=_vmem_limit(need)),
        name="hgrn2",
    )(seq3(hq), seq3(hlf), seq3(hk), seq3(hv), seq3(hg), out_norm, ltri2, bd256)
    return out.reshape(bsz * length, HG_W)


def _out_mlp_body(n_views, *refs):
    wo_ref, nw_ref, wup_ref, wdn_ref, out_ref = refs[4 * n_views:]

    def rows(k):
        return jnp.concatenate([r[...] for r in refs[k * n_views:(k + 1) * n_views]], axis=0)

    h1 = (rows(3)
          + _dot(rows(0), wo_ref[0:SB_W, :])
          + _dot(rows(1), wo_ref[SB_W:SB_W + SSD_W, :])
          + _dot(rows(2), wo_ref[SB_W + SSD_W:, :]))
    ms = jnp.mean(h1 * h1, axis=-1, keepdims=True)
    hn = (h1 * lax.rsqrt(ms + EPS) * nw_ref[...]).astype(BF16)
    mlp = jnp.zeros_like(h1)
    for c in range(wup_ref.shape[1] // FF_BLOCK):
        u = _dot(hn, wup_ref[:, c * FF_BLOCK:(c + 1) * FF_BLOCK])
        act = jnp.square(jnp.maximum(u, 0.0)).astype(BF16)
        mlp = mlp + _dot(act, wdn_ref[c * FF_BLOCK:(c + 1) * FF_BLOCK, :])
    out_ref[...] = h1 + mlp


def _out_mlp(o_sb, o_ssd, o_hg, h, w_out, norm_w, w_up, w_down, drop_lead=None):
    rows, d = h.shape
    d_ff = w_up.shape[1]
    const = lambda i: (0, 0)
    operands = (o_sb, o_ssd, o_hg, h)
    if drop_lead is None:
        tm = _row_tile(rows, ROW_TILE)
        n_views, grid = 1, rows // tm
        row_args = list(operands)
        row_specs = [pl.BlockSpec((tm, t.shape[1]), lambda i: (i, 0)) for t in operands]
        out_spec = pl.BlockSpec((tm, d), lambda i: (i, 0))
        out_shape = jax.ShapeDtypeStruct((rows, d), F32)
    else:
        bsz, length = drop_lead
        chunks_seq = length // CHUNK
        tm = _row_tile(length - CHUNK, ROW_TILE)
        n_views = tm // CHUNK
        tiles_per_seq = (length - CHUNK) // tm

        def view(width, m):
            return pl.BlockSpec((None, CHUNK, width), lambda i: (
                (i // tiles_per_seq) * chunks_seq + 1 + (i % tiles_per_seq) * n_views + m, 0, 0))

        grid = bsz * tiles_per_seq
        row_args = [t.reshape(rows // CHUNK, CHUNK, t.shape[1]) for t in operands for _ in range(n_views)]
        row_specs = [view(t.shape[1], m) for t in operands for m in range(n_views)]
        out_spec = pl.BlockSpec((None, tm, d), lambda i: (i // tiles_per_seq, i % tiles_per_seq, 0))
        out_shape = jax.ShapeDtypeStruct((bsz, length - CHUNK, d), F32)
    weights = (w_out.shape[0] * d + 2 * d * d_ff) * 2
    need = 2 * weights + 2 * tm * (2 * d * 4 + (SB_W + SSD_W + HG_W) * 2) + 6 * tm * FF_BLOCK * 4
    return pl.pallas_call(
        functools.partial(_out_mlp_body, n_views),
        grid=(grid,),
        in_specs=row_specs + [
            pl.BlockSpec(w_out.shape, const),
            pl.BlockSpec((1, d), const),
            pl.BlockSpec(w_up.shape, const),
            pl.BlockSpec(w_down.shape, const),
        ],
        out_specs=out_spec,
        out_shape=out_shape,
        compiler_params=pltpu.CompilerParams(
            dimension_semantics=("parallel",), vmem_limit_bytes=_vmem_limit(need)),
        name="out_mlp",
    )(*row_args, w_out, norm_w, w_up, w_down)


def _block_diag_ones(n, block):
    idx = np.arange(n) // block
    return jnp.asarray(idx[:, None] == idx[None, :], BF16)


def _constants():
    t = np.arange(CHUNK)
    ltri = jnp.asarray(t[None, :] <= t[:, None], BF16)
    later = (t[:, None] > t[None, :])
    usum = jnp.asarray(np.concatenate([later, np.ones((CHUNK, CHUNK), bool)], axis=1), BF16)
    eexp = np.zeros((DT_W, SSD_W), bool)
    for h in range(SSD_HEADS):
        eexp[h, h * HEAD_DIM:(h + 1) * HEAD_DIM] = True
    in_sub = (t[None, :] <= t[:, None]) & (t[None, :] // SUB == t[:, None] // SUB)
    ltri2 = jnp.asarray(np.concatenate([t[None, :] <= t[:, None], in_sub], axis=0), BF16)
    shift = jnp.asarray(np.concatenate([t[None, :] == t[:, None] - k for k in range(1, SSD_CONV)], axis=0), BF16)
    return ltri, ltri2, usum, jnp.asarray(eexp, BF16), shift


def _pack_w_in(w):
    d = w.shape[0]
    dt_lo = 3 * SB_W + 2 * SSD_W + 2 * SSD_BC_W
    dt_hi = dt_lo + SSD_HEADS
    return jnp.concatenate(
        [w[:, :dt_lo], w[:, dt_lo:dt_hi], jnp.zeros((d, DT_W - SSD_HEADS), w.dtype), w[:, dt_hi:]],
        axis=1).astype(BF16)


def _pad_lanes(v, width):
    return jnp.pad(v.astype(F32), (0, width - v.shape[0]))[None, :]


def kernel(x, meta_tokens, hg_lb_logits, norm_mix_w, w_in, sb_q_norm, sb_k_norm, sb_out_norm, ssd_conv_w,
           ssd_conv_b, ssd_dt_bias, ssd_A_log, ssd_D, ssd_norm_w, hg_out_norm, w_out, norm_mlp_w, w_up, w_down):
    bsz, seq, d = x.shape
    depth = w_in.shape[0]
    length = seq + CHUNK
    lead = jnp.concatenate([jnp.zeros((PAD, d), x.dtype), meta_tokens.astype(x.dtype)], axis=0)
    h = x

    probs = jax.nn.softmax(hg_lb_logits.astype(F32), axis=0)
    lbs = jnp.concatenate([jnp.zeros_like(probs[0:1]), jnp.cumsum(probs, axis=0)[:-1]], axis=0)

    ltri, ltri2, usum, eexp, shift = _constants()
    bd256 = _block_diag_ones(SB_W, HEAD_DIM)
    bd128 = _block_diag_ones(LANES, HEAD_DIM)

    for l in range(depth):
        proj = _in_proj(
            h, norm_mix_w[l][None, :], _pack_w_in(w_in[l]),
            jnp.tile(sb_q_norm[l], SB_HEADS)[None, :], jnp.tile(sb_k_norm[l], SB_HEADS)[None, :], bd256,
            _pad_lanes(ssd_dt_bias[l], DT_W), lbs[l][None, :], length, lead if l == 0 else None)
        if l == 0:
            h = proj[-1]
        q, k, v, zg, xbc, dt, hq, hlf, hk, hv, hg = proj[:11]
        o_sb = _sb_attn(q, k, v, sb_out_norm[l].reshape(1, SB_W), usum, bd128, bsz, length)
        o_ssd = _ssd(zg, xbc, dt, ssd_conv_w[l], ssd_conv_b[l][None, :],
                     _pad_lanes(ssd_A_log[l], DT_W), jnp.repeat(ssd_D[l].astype(F32), HEAD_DIM)[None, :],
                     ssd_norm_w[l].reshape(1, SSD_W), ltri, eexp, shift, bsz, length)
        o_hg = _hgrn2(hq, hlf, hk, hv, hg, hg_out_norm[l].reshape(1, HG_W), ltri2, bd256, bsz, length)
        h = _out_mlp(o_sb, o_ssd, o_hg, h, w_out[l].astype(BF16), norm_mlp_w[l][None, :],
                     w_up[l].astype(BF16), w_down[l].astype(BF16), (bsz, length) if l == depth - 1 else None)
    return h
```
